```python
import math
import jax
import jax.numpy as jnp
from jax import lax
import numpy as np

D_MODEL = 2048
BATCH = 4
SEQ = 8192
DEPTH = 2

GRID_W = 64
CTX_LEN = 256

MLSTM_HEADS = 4
MLSTM_HD = 128
MLSTM_W = MLSTM_HEADS * MLSTM_HD
MLSTM_CHUNK = 64
MLSTM_CONV = 3
CONV_W = 512
CONV_K = 31
GQA_HEADS = 4
GQA_KV_HEADS = 2
GQA_HD = 128
DIFF_HEADS = 4
DIFF_HD = 64
DIFF_VD = 2 * DIFF_HD

BRANCH_W = 512
N_BRANCHES = 4
Q_BLOCK = 128
ROPE_THETA = 10000.0

D_FF = 5632
N_EXPERTS = 8
TOP_K = 2
D_FF_EXPERT = 7168
MOE_BLOCK = 128

EPS = 1e-6

IN_SPLITS = (2 * MLSTM_W, MLSTM_W, MLSTM_W, 4 * MLSTM_HEADS, 2 * CONV_W, GQA_HEADS * GQA_HD, GQA_KV_HEADS * GQA_HD, GQA_KV_HEADS * GQA_HD, DIFF_HEADS * 2 * DIFF_HD, DIFF_HEADS * 2 * DIFF_HD, DIFF_HEADS * DIFF_VD)
D_IN = sum(IN_SPLITS)

kernel_name = 'hybrid_flow_block'


def rms_norm(x, g):
    xf = x.astype(jnp.float32)
    y = xf * lax.rsqrt(jnp.mean(jnp.square(xf), axis=-1, keepdims=True) + EPS)
    return (y * g.astype(jnp.float32)).astype(x.dtype)


def layer_norm(x, g, b):
    xf = x.astype(jnp.float32)
    mu = jnp.mean(xf, axis=-1, keepdims=True)
    xc = xf - mu
    y = xc * lax.rsqrt(jnp.mean(jnp.square(xc), axis=-1, keepdims=True) + EPS)
    return (y * g.astype(jnp.float32) + b.astype(jnp.float32)).astype(x.dtype)


def modulate(h, shift, scale):
    return h * (1.0 + scale) + shift


def split_cols(z):
    return jnp.split(z, np.cumsum(IN_SPLITS)[:-1].tolist(), axis=-1)


def split_heads(a, n_heads):
    B, T, W = a.shape
    return a.reshape(B, T, n_heads, W // n_heads).transpose(0, 2, 1, 3)


def merge_heads(a):
    B, H, T, d = a.shape
    return a.transpose(0, 2, 1, 3).reshape(B, T, H * d)


def depthwise_conv(x, w, b):
    pad = w.shape[0] // 2
    y = lax.conv_general_dilated(x, w[:, None, :], window_strides=(1,), padding=[(pad, w.shape[0] - 1 - pad)], dimension_numbers=('NWC', 'WIO', 'NWC'), feature_group_count=x.shape[-1])
    return y + b


def axial_angles(rows, cols, head_dim):
    r = head_dim // 4
    freqs = ROPE_THETA ** (-jnp.arange(r, dtype=jnp.float32) / r)
    return jnp.concatenate([rows[:, None].astype(jnp.float32) * freqs, cols[:, None].astype(jnp.float32) * freqs], axis=-1)


def axial_rope(x, ang):
    *lead, T, d = x.shape
    r = d // 4
    xs = x.astype(jnp.float32).reshape(*lead, T, 2, 2, r)
    a, b = xs[..., 0, :], xs[..., 1, :]
    ang = ang.reshape(T, 2, r)
    cos, sin = jnp.cos(ang), jnp.sin(ang)
    out = jnp.stack([a * cos - b * sin, a * sin + b * cos], axis=-2)
    return out.reshape(x.shape).astype(x.dtype)


def attend_latent(q, k, v, k_ctx, v_ctx):
    B, Hq, S, d = q.shape
    Hk = k.shape[1]
    G = Hq // Hk
    nb = S // Q_BLOCK
    scale = d ** -0.5
    qb = q.reshape(B, Hk, G, nb, Q_BLOCK, d).transpose(3, 0, 1, 2, 4, 5)

    def block(qi):
        s = jnp.concatenate([jnp.einsum('bhgqd,bhkd->bhgqk', qi, k), jnp.einsum('bhgqd,bhkd->bhgqk', qi, k_ctx)], axis=-1)
        p = jax.nn.softmax(s.astype(jnp.float32) * scale, axis=-1).astype(v.dtype)
        return jnp.einsum('bhgqk,bhkv->bhgqv', p[..., :S], v) + jnp.einsum('bhgqk,bhkv->bhgqv', p[..., S:], v_ctx)

    o = lax.map(block, qb)
    return o.transpose(1, 2, 3, 0, 4, 5).reshape(B, Hq, S, v.shape[-1])


def attend_ctx(q, k, v):
    B, Hq, T, d = q.shape
    Hk = k.shape[1]
    qg = q.reshape(B, Hk, Hq // Hk, T, d)
    s = jnp.einsum('bhgqd,bhkd->bhgqk', qg, k).astype(jnp.float32) * d ** -0.5
    p = jax.nn.softmax(s, axis=-1).astype(v.dtype)
    return jnp.einsum('bhgqk,bhkv->bhgqv', p, v).reshape(B, Hq, T, v.shape[-1])


def mlstm_chunkwise(q, k, v, log_i, log_f):
    f32 = jnp.float32
    q, k, v, log_i, log_f = (a.astype(f32) for a in (q, k, v, log_i, log_f))
    B, H, T, dh = q.shape
    n_chunks = T // MLSTM_CHUNK

    def chunks(a):
        return jnp.moveaxis(a.reshape(B, H, n_chunks, MLSTM_CHUNK, *a.shape[3:]), 2, 0)

    lower = jnp.tril(jnp.ones((MLSTM_CHUNK, MLSTM_CHUNK), dtype=bool))

    def step(carry, blk):
        c_state, n_state, m_state = carry
        qc, kc, vc, ic, fc = blk
        b = jnp.cumsum(fc, axis=-1)
        inter = b + m_state[..., None]
        d_log = jnp.where(lower, b[..., :, None] - b[..., None, :] + ic[..., None, :], -jnp.inf)
        m_t = jnp.maximum(inter, jnp.max(d_log, axis=-1))
        w = jnp.exp(d_log - m_t[..., None]) * jnp.einsum('bhtd,bhsd->bhts', qc, kc)
        a_inter = jnp.exp(inter - m_t)
        num = a_inter[..., None] * jnp.einsum('bhvk,bhtk->bhtv', c_state, qc) + jnp.einsum('bhts,bhsv->bhtv', w, vc)
        den = a_inter * jnp.einsum('bhk,bhtk->bht', n_state, qc) + jnp.sum(w, axis=-1)
        h = num / jnp.maximum(jnp.abs(den), jnp.exp(-m_t))[..., None]
        b_end = b[..., -1]
        g = b_end[..., None] - b + ic
        m_new = jnp.maximum(b_end + m_state, jnp.max(g, axis=-1))
        decay = jnp.exp(b_end + m_state - m_new)
        wg = jnp.exp(g - m_new[..., None])
        c_state = decay[..., None, None] * c_state + jnp.einsum('bhs,bhsv,bhsk->bhvk', wg, vc, kc)
        n_state = decay[..., None] * n_state + jnp.einsum('bhs,bhsk->bhk', wg, kc)
        return (c_state, n_state, m_new), h

    init = (jnp.zeros((B, H, dh, dh), f32), jnp.zeros((B, H, dh), f32), jnp.zeros((B, H), f32))
    _, hs = lax.scan(step, init, tuple(chunks(a) for a in (q, k, v, log_i, log_f)))
    return jnp.moveaxis(hs, 0, 2).reshape(B, H, T, dh)


def mlstm_inputs(z, conv_w, conv_b, gate_b):
    qk = jax.nn.silu(depthwise_conv(z[0], conv_w, conv_b))
    q, k = jnp.split(qk, 2, axis=-1)
    B, T, _ = q.shape
    gates = (z[3] + gate_b).astype(jnp.float32).reshape(B, T, 4, MLSTM_HEADS).transpose(2, 0, 3, 1)
    log_gates = (gates[0], jax.nn.log_sigmoid(gates[1]), gates[2], jax.nn.log_sigmoid(gates[3]))
    return split_heads(q, MLSTM_HEADS), split_heads(k, MLSTM_HEADS) * MLSTM_HD ** -0.5, split_heads(z[1], MLSTM_HEADS), log_gates


def mlstm_output(h, o_pre, norm_g):
    hn = rms_norm(h, norm_g.reshape(MLSTM_HEADS, 1, MLSTM_HD))
    return merge_heads(hn).astype(o_pre.dtype) * jax.nn.sigmoid(o_pre)


def conformer_conv(glu_pre, w, b, ln_g, ln_b):
    a, gt = jnp.split(glu_pre, 2, axis=-1)
    u = depthwise_conv(a * jax.nn.sigmoid(gt), w, b)
    return jax.nn.silu(layer_norm(u, ln_g, ln_b))


def gqa_heads(z, q_g, k_g):
    q = rms_norm(split_heads(z[5], GQA_HEADS), q_g)
    k = rms_norm(split_heads(z[6], GQA_KV_HEADS), k_g)
    return q, k, split_heads(z[7], GQA_KV_HEADS)


def diff_heads(z, q_g, k_g):
    B, T, _ = z[8].shape
    q = rms_norm(z[8].reshape(B, T, DIFF_HEADS, 2, DIFF_HD), q_g).transpose(3, 0, 2, 1, 4)
    k = rms_norm(z[9].reshape(B, T, DIFF_HEADS, 2, DIFF_HD), k_g).transpose(3, 0, 2, 1, 4)
    return q[0], q[1], k[0], k[1], split_heads(z[10], DIFF_HEADS)


def diff_output(o, sub_g, lam_init):
    return merge_heads(rms_norm(o, sub_g) * (1.0 - lam_init))


def merge_branches(h, branches, mg_w, mg_b, br_w, o_w):
    y = None
    for j, o in enumerate(branches):
        term = jax.nn.sigmoid(h @ mg_w[j] + mg_b[j]) * (o.astype(h.dtype) @ br_w[j])
        y = term if y is None else y + term
    return y @ o_w


def hybrid_mixer(h_lat, h_ctx, need_ctx, ang_gqa, ang_diff, lam_init, w_in, m_conv_w, m_conv_b, m_gate_b, m_norm_g, c_dw_w, c_dw_b, c_ln_g, c_ln_b, g_q_g, g_k_g, d_q_g, d_k_g, d_lam, d_sub_g, mg_w, mg_b, br_w, o_w):
    L = h_ctx.shape[1]
    zl = split_cols(h_lat @ w_in)
    zc = split_cols(h_ctx @ w_in)
    cat = lambda a, b: jnp.concatenate([a, b], axis=2)
    rev = lambda a: jnp.flip(a, axis=2)

    ql, kl, vl, gl = mlstm_inputs(zl, m_conv_w, m_conv_b, m_gate_b)
    qc, kc, vc, gc = mlstm_inputs(zc, m_conv_w, m_conv_b, m_gate_b)
    h_fwd = mlstm_chunkwise(cat(qc, ql), cat(kc, kl), cat(vc, vl), cat(gc[0], gl[0]), cat(gc[1], gl[1]))
    h_bwd = mlstm_chunkwise(cat(rev(qc), rev(ql)), cat(rev(kc), rev(kl)), cat(rev(vc), rev(vl)), cat(rev(gc[2]), rev(gl[2])), cat(rev(gc[3]), rev(gl[3])))
    a_lat = mlstm_output(h_fwd[:, :, L:] + rev(h_bwd[:, :, L:]), zl[2], m_norm_g)

    b_lat = conformer_conv(zl[4], c_dw_w, c_dw_b, c_ln_g, c_ln_b)

    gq_l, gk_l, gv_l = gqa_heads(zl, g_q_g, g_k_g)
    gq_c, gk_c, gv_c = gqa_heads(zc, g_q_g, g_k_g)
    c_lat = merge_heads(attend_latent(axial_rope(gq_l, ang_gqa), axial_rope(gk_l, ang_gqa), gv_l, gk_c, gv_c))

    dl = d_lam.astype(jnp.float32)
    lam = jnp.exp(jnp.sum(dl[0] * dl[1])) - jnp.exp(jnp.sum(dl[2] * dl[3])) + lam_init
    dq1_l, dq2_l, dk1_l, dk2_l, dv_l = diff_heads(zl, d_q_g, d_k_g)
    dq1_c, dq2_c, dk1_c, dk2_c, dv_c = diff_heads(zc, d_q_g, d_k_g)
    o1 = attend_latent(axial_rope(dq1_l, ang_diff), axial_rope(dk1_l, ang_diff), dv_l, dk1_c, dv_c)
    o2 = attend_latent(axial_rope(dq2_l, ang_diff), axial_rope(dk2_l, ang_diff), dv_l, dk2_c, dv_c)
    d_lat = diff_output(o1 - lam * o2, d_sub_g, lam_init)

    y_lat = merge_branches(h_lat, (a_lat, b_lat, c_lat, d_lat), mg_w, mg_b, br_w, o_w)
    if not need_ctx:
        return y_lat, None

    a_ctx = mlstm_output(h_fwd[:, :, :L] + rev(h_bwd[:, :, :L]), zc[2], m_norm_g)
    b_ctx = conformer_conv(zc[4], c_dw_w, c_dw_b, c_ln_g, c_ln_b)
    c_ctx_o = merge_heads(attend_ctx(gq_c, gk_c, gv_c))
    d_ctx = diff_output(attend_ctx(dq1_c, dk1_c, dv_c) - lam * attend_ctx(dq2_c, dk2_c, dv_c), d_sub_g, lam_init)
    y_ctx = merge_branches(h_ctx, (a_ctx, b_ctx, c_ctx_o, d_ctx), mg_w, mg_b, br_w, o_w)
    return y_lat, y_ctx


def swiglu(h, w1, w3, w2):
    return (jax.nn.silu(h @ w1) * (h @ w3)) @ w2


def moe_swiglu(t, router, w1, w3, w2):
    N = t.shape[0]
    n_assign = N * TOP_K
    logits = (t @ router).astype(jnp.float32)
    top_v, top_e = lax.top_k(logits, TOP_K)
    top_w = jax.nn.softmax(top_v, axis=-1)
    e_flat = top_e.reshape(-1)
    w_flat = top_w.reshape(-1)
    tok_flat = jnp.repeat(jnp.arange(N, dtype=jnp.int32), TOP_K)
    order = jnp.argsort(e_flat)
    e_s, tok_s, w_s = e_flat[order], tok_flat[order], w_flat[order]
    counts = jnp.bincount(e_flat, length=N_EXPERTS)
    padded = (counts + MOE_BLOCK - 1) // MOE_BLOCK * MOE_BLOCK
    pad_end = jnp.cumsum(padded)
    pad_start = pad_end - padded
    start = jnp.cumsum(counts) - counts
    dest = pad_start[e_s] + jnp.arange(n_assign, dtype=jnp.int32) - start[e_s]
    n_blocks = -(-n_assign // MOE_BLOCK) + N_EXPERTS
    P = n_blocks * MOE_BLOCK
    buf_tok = jnp.zeros((P,), jnp.int32).at[dest].set(tok_s)
    buf_w = jnp.zeros((P,), jnp.float32).at[dest].set(w_s)
    blk_e = jnp.minimum(jnp.searchsorted(pad_end, jnp.arange(n_blocks, dtype=jnp.int32) * MOE_BLOCK, side='right'), N_EXPERTS - 1)

    def step(y, blk):
        idx, wt, e = blk
        xb = t[idx]
        hb = jax.nn.silu(xb @ w1[e]) * (xb @ w3[e])
        return y.at[idx].add((hb @ w2[e]) * wt[:, None].astype(t.dtype)), None

    y, _ = lax.scan(step, jnp.zeros_like(t), (buf_tok.reshape(n_blocks, MOE_BLOCK), buf_w.reshape(n_blocks, MOE_BLOCK), blk_e))
    return y


def setup_inputs(seed: int = 0) -> dict:
    keys = iter(jax.random.split(jax.random.key(seed), 64))
    f32 = jnp.float32
    D = D_MODEL
    n_dense = (DEPTH + 1) // 2
    n_moe = DEPTH // 2

    def normal(shape, scale):
        return jax.random.normal(next(keys), shape, f32) * scale

    def gain(shape):
        return 1.0 + normal(shape, 0.02)

    f_bias = jnp.linspace(3.0, 6.0, MLSTM_HEADS, dtype=f32)
    mlstm_gate_b = jnp.concatenate([normal((DEPTH, MLSTM_HEADS), 0.1), f_bias + normal((DEPTH, MLSTM_HEADS), 0.1), normal((DEPTH, MLSTM_HEADS), 0.1), f_bias + normal((DEPTH, MLSTM_HEADS), 0.1)], axis=-1)
    return {
        'x': normal((BATCH, SEQ, D), 1.0),
        'c': normal((BATCH, D), 1.0),
        'ctx': normal((BATCH, CTX_LEN, D), 1.0),
        'c_ctx': normal((D,), 1.0),
        'ada_w': normal((DEPTH, D, 6 * D), 0.5 * D ** -0.5),
        'ada_b': normal((DEPTH, 6 * D), 0.01),
        'norm1_g': gain((DEPTH, D)),
        'norm2_g': gain((DEPTH, D)),
        'w_in': normal((DEPTH, D, D_IN), D ** -0.5),
        'mlstm_conv_w': normal((DEPTH, MLSTM_CONV, 2 * MLSTM_W), MLSTM_CONV ** -0.5),
        'mlstm_conv_b': normal((DEPTH, 2 * MLSTM_W), 0.01),
        'mlstm_gate_b': mlstm_gate_b,
        'mlstm_norm_g': gain((DEPTH, MLSTM_W)),
        'conv_dw_w': normal((DEPTH, CONV_K, CONV_W), CONV_K ** -0.5),
        'conv_dw_b': normal((DEPTH, CONV_W), 0.01),
        'conv_ln_g': gain((DEPTH, CONV_W)),
        'conv_ln_b': normal((DEPTH, CONV_W), 0.01),
        'gqa_q_norm_g': gain((DEPTH, GQA_HD)),
        'gqa_k_norm_g': gain((DEPTH, GQA_HD)),
        'diff_q_norm_g': gain((DEPTH, DIFF_HD)),
        'diff_k_norm_g': gain((DEPTH, DIFF_HD)),
        'diff_lambda': normal((DEPTH, 4, DIFF_HD), 0.1),
        'diff_subln_g': gain((DEPTH, DIFF_VD)),
        'merge_gate_w': normal((DEPTH, N_BRANCHES, D, D), D ** -0.5),
        'merge_gate_b': normal((DEPTH, N_BRANCHES, D), 0.01),
        'branch_w': normal((DEPTH, N_BRANCHES, BRANCH_W, D), BRANCH_W ** -0.5),
        'out_w': normal((DEPTH, D, D), D ** -0.5),
        'ffn_w1': normal((n_dense, D, D_FF), D ** -0.5),
        'ffn_w3': normal((n_dense, D, D_FF), D ** -0.5),
        'ffn_w2': normal((n_dense, D_FF, D), D_FF ** -0.5),
        'moe_router': normal((n_moe, D, N_EXPERTS), D ** -0.5),
        'moe_w1': normal((n_moe, N_EXPERTS, D, D_FF_EXPERT), D ** -0.5),
        'moe_w3': normal((n_moe, N_EXPERTS, D, D_FF_EXPERT), D ** -0.5),
        'moe_w2': normal((n_moe, N_EXPERTS, D_FF_EXPERT, D), D_FF_EXPERT ** -0.5),
    }


def reference(x, c, ctx, c_ctx, ada_w, ada_b, norm1_g, norm2_g, w_in, mlstm_conv_w, mlstm_conv_b, mlstm_gate_b, mlstm_norm_g, conv_dw_w, conv_dw_b, conv_ln_g, conv_ln_b, gqa_q_norm_g, gqa_k_norm_g, diff_q_norm_g, diff_k_norm_g, diff_lambda, diff_subln_g, merge_gate_w, merge_gate_b, branch_w, out_w, ffn_w1, ffn_w3, ffn_w2, moe_router, moe_w1, moe_w3, moe_w2):
    B, S, D = x.shape
    L = ctx.shape[1]
    ROWS = S // GRID_W
    rows = jnp.repeat(jnp.arange(ROWS, dtype=jnp.int32), GRID_W)
    cols = jnp.tile(jnp.arange(GRID_W, dtype=jnp.int32), ROWS)
    ang_gqa = axial_angles(rows, cols, GQA_HD)
    ang_diff = axial_angles(rows, cols, DIFF_HD)
    silu_c = jax.nn.silu(c)
    silu_cc = jax.nn.silu(c_ctx)
    x_lat, x_ctx = x, ctx
    for l in range(DEPTH):
        need_ctx = l < DEPTH - 1
        mod_l = jnp.split((silu_c @ ada_w[l] + ada_b[l])[:, None, :], 6, axis=-1)
        mod_c = jnp.split(silu_cc @ ada_w[l] + ada_b[l], 6, axis=-1)
        lam_init = 0.8 - 0.6 * math.exp(-0.3 * l)

        h_lat = modulate(rms_norm(x_lat, norm1_g[l]), mod_l[0], mod_l[1])
        h_ctx = modulate(rms_norm(x_ctx, norm1_g[l]), mod_c[0], mod_c[1])
        y_lat, y_ctx = hybrid_mixer(h_lat, h_ctx, need_ctx, ang_gqa, ang_diff, lam_init, w_in[l], mlstm_conv_w[l], mlstm_conv_b[l], mlstm_gate_b[l], mlstm_norm_g[l], conv_dw_w[l], conv_dw_b[l], conv_ln_g[l], conv_ln_b[l], gqa_q_norm_g[l], gqa_k_norm_g[l], diff_q_norm_g[l], diff_k_norm_g[l], diff_lambda[l], diff_subln_g[l], merge_gate_w[l], merge_gate_b[l], branch_w[l], out_w[l])
        x_lat = x_lat + mod_l[2] * y_lat
        if need_ctx:
            x_ctx = x_ctx + mod_c[2] * y_ctx

        h_lat = modulate(rms_norm(x_lat, norm2_g[l]), mod_l[3], mod_l[4])
        i = l // 2
        if l % 2 == 0:
            x_lat = x_lat + mod_l[5] * swiglu(h_lat, ffn_w1[i], ffn_w3[i], ffn_w2[i])
            if need_ctx:
                h_ctx = modulate(rms_norm(x_ctx, norm2_g[l]), mod_c[3], mod_c[4])
                x_ctx = x_ctx + mod_c[5] * swiglu(h_ctx, ffn_w1[i], ffn_w3[i], ffn_w2[i])
        else:
            tokens = h_lat.reshape(B * S, D)
            if need_ctx:
                h_ctx = modulate(rms_norm(x_ctx, norm2_g[l]), mod_c[3], mod_c[4])
                tokens = jnp.concatenate([tokens, h_ctx.reshape(B * L, D)], axis=0)
            f_out = moe_swiglu(tokens, moe_router[i], moe_w1[i], moe_w3[i], moe_w2[i])
            x_lat = x_lat + mod_l[5] * f_out[:B * S].reshape(B, S, D)
            if need_ctx:
                x_ctx = x_ctx + mod_c[5] * f_out[B * S:].reshape(B, L, D)
    return x_lat
```

```python
import functools
import math

import numpy as np
import jax
import jax.numpy as jnp
from jax import lax
from jax.experimental import pallas as pl
from jax.experimental.pallas import tpu as pltpu

F32 = jnp.float32
BF16 = jnp.bfloat16

GRID_W = 64
M_HEADS, M_HD, M_CHUNK, M_CONV = 4, 128, 64, 3
M_W = M_HEADS * M_HD
C_W, C_K = 512, 31
G_HEADS, G_KV, G_HD = 4, 2, 128
X_HEADS, X_HD = 4, 64
X_VD = 2 * X_HD
ROPE_THETA = 10000.0
N_EXPERTS, TOP_K = 8, 2
EPS = 1e-6
LANE = 128

_REF_SPLITS = (2 * M_W, M_W, M_W, 4 * M_HEADS, 2 * C_W, G_HEADS * G_HD, G_KV * G_HD, G_KV * G_HD,
               X_HEADS * 2 * X_HD, X_HEADS * 2 * X_HD, X_HEADS * X_VD)
_REF_OFF = np.concatenate([[0], np.cumsum(_REF_SPLITS)]).tolist()
_Z = {
    "m_qk": (0, 0, 1024), "c_glu": (4, 1024, 1024), "m_v": (1, 2048, 512), "m_o": (2, 2560, 512),
    "g_q": (5, 3072, 512), "x_q": (8, 3584, 512), "x_k": (9, 4096, 512), "x_v": (10, 4608, 512),
    "g_k": (6, 5120, 256), "g_v": (7, 5376, 256), "m_g": (3, 5632, 128),
}
Z_W = 5760

_VMEM_LIMIT = 56 * 1024 * 1024


def _cp(sem, vmem=_VMEM_LIMIT):
    return pltpu.CompilerParams(dimension_semantics=sem, vmem_limit_bytes=vmem)


def _pick(n, cands):
    for c in cands:
        if n % c == 0:
            return c
    raise ValueError(f"no tile for {n} in {cands}")


def _sigmoid(x):
    return 1.0 / (1.0 + jnp.exp(-x))


def _silu(x):
    return x * _sigmoid(x)


def _adaln_kernel(c_ref, w_ref, b_ref, o_ref):
    s = _silu(c_ref[...]).astype(BF16)
    o_ref[...] = jnp.dot(s, w_ref[...].astype(BF16), preferred_element_type=F32) + b_ref[...]


def adaln(cc, w, b):
    R, D = cc.shape
    N = w.shape[1]
    tn = _pick(N, (512, 256, 128))
    return pl.pallas_call(
        _adaln_kernel,
        grid=(N // tn,),
        in_specs=[pl.BlockSpec((R, D), lambda n: (0, 0)),
                  pl.BlockSpec((D, tn), lambda n: (0, n)),
                  pl.BlockSpec((1, tn), lambda n: (0, n))],
        out_specs=pl.BlockSpec((R, tn), lambda n: (0, n)),
        out_shape=jax.ShapeDtypeStruct((R, N), F32),
        compiler_params=_cp(("parallel",)),
        name="adaln",
    )(cc, w, b.reshape(1, N))


class Geo:
    def __init__(self, B, S, L, D):
        self.B, self.S, self.L, self.D = B, S, L, D
        self.BS = B * S
        self.N = B * S + B * L

    def group(self, row0):
        return jnp.minimum(row0 // self.S, self.B)


def _norm_mod_kernel(x_ref, g_ref, sh_ref, sc_ref, o_ref):
    x = x_ref[...]
    y = x * lax.rsqrt(jnp.mean(x * x, axis=-1, keepdims=True) + EPS)
    y = y * g_ref[...]
    o_ref[...] = (y * (1.0 + sc_ref[...]) + sh_ref[...]).astype(o_ref.dtype)


def norm_mod(x, g, mod4, geo, i_shift, i_scale, rows, out_dtype):
    D = geo.D
    tm = _pick(math.gcd(geo.S, geo.B * geo.L), (256, 128, 64))
    return pl.pallas_call(
        _norm_mod_kernel,
        grid=(rows // tm,),
        in_specs=[pl.BlockSpec((tm, D), lambda m: (m, 0)),
                  pl.BlockSpec((1, D), lambda m: (0, 0)),
                  pl.BlockSpec((None, None, 1, D), lambda m: (geo.group(m * tm), i_shift, 0, 0)),
                  pl.BlockSpec((None, None, 1, D), lambda m: (geo.group(m * tm), i_scale, 0, 0))],
        out_specs=pl.BlockSpec((tm, D), lambda m: (m, 0)),
        out_shape=jax.ShapeDtypeStruct((rows, D), out_dtype),
        compiler_params=_cp(("parallel",)),
        name="norm_mod",
    )(x, g.reshape(1, D), mod4, mod4)


def _mm_body(a_ref, w_refs, o_ref, acc_refs, nk, finalize):
    k = pl.program_id(2)
    a = a_ref[...]
    if a.dtype != BF16:
        a = a.astype(BF16)
    parts = [jnp.dot(a, w[...], preferred_element_type=F32) for w in w_refs]
    if nk == 1:
        finalize(parts)
        return

    @pl.when(k == 0)
    def _():
        for acc, p in zip(acc_refs, parts):
            acc[...] = p

    @pl.when(k > 0)
    def _():
        for acc, p in zip(acc_refs, parts):
            acc[...] += p

    @pl.when(k == nk - 1)
    def _():
        finalize([acc[...] for acc in acc_refs])


def _dense_mm_kernel(*refs, n_w, nk, epi):
    refs = list(refs)
    a_ref = refs.pop(0)
    w_refs = [refs.pop(0) for _ in range(n_w)]
    extra = []
    if epi == "resid":
        extra = [refs.pop(0), refs.pop(0)]
    o_ref = refs.pop(0)

    def finalize(parts):
        if epi == "none":
            o_ref[...] = parts[0].astype(o_ref.dtype)
        elif epi == "swiglu":
            o_ref[...] = (_silu(parts[0]) * parts[1]).astype(o_ref.dtype)
        else:
            o_ref[...] = (extra[0][...] + extra[1][...] * parts[0]).astype(o_ref.dtype)

    _mm_body(a_ref, w_refs, o_ref, refs, nk, finalize)


def dense_mm(a, ws, *, rows, tm, tn, tk, order, epi, out_dtype, geo=None, res=None, mod4=None, gate_idx=None, name="mm"):
    K = a.shape[1]
    N = ws[0].shape[1]
    nm, nn, nk = rows // tm, N // tn, K // tk
    assert rows % tm == 0 and N % tn == 0 and K % tk == 0
    if order == "mn":
        grid = (nm, nn, nk)
        mi = lambda g0, g1: g0
        ni = lambda g0, g1: g1
    else:
        grid = (nn, nm, nk)
        mi = lambda g0, g1: g1
        ni = lambda g0, g1: g0
    in_specs = [pl.BlockSpec((tm, tk), lambda g0, g1, k: (mi(g0, g1), k))]
    in_specs += [pl.BlockSpec((tk, tn), lambda g0, g1, k: (k, ni(g0, g1))) for _ in ws]
    args = [a, *ws]
    if epi == "resid":
        in_specs.append(pl.BlockSpec((tm, tn), lambda g0, g1, k: (mi(g0, g1), ni(g0, g1))))
        in_specs.append(pl.BlockSpec((None, None, 1, tn),
                                     lambda g0, g1, k: (geo.group(mi(g0, g1) * tm), gate_idx, 0, ni(g0, g1))))
        args += [res, mod4]
    scratch = [pltpu.VMEM((tm, tn), F32) for _ in ws] if nk > 1 else []
    return pl.pallas_call(
        functools.partial(_dense_mm_kernel, n_w=len(ws), nk=nk, epi=epi),
        grid=grid,
        in_specs=in_specs,
        out_specs=pl.BlockSpec((tm, tn), lambda g0, g1, k: (mi(g0, g1), ni(g0, g1))),
        out_shape=jax.ShapeDtypeStruct((rows, N), out_dtype),
        scratch_shapes=scratch,
        compiler_params=_cp(("parallel", "parallel", "arbitrary")),
        name=name,
    )(*args)


def _grouped_mm_kernel(te_ref, nu_ref, *refs, n_w, nk, epi):
    refs = list(refs)
    a_ref = refs.pop(0)
    w_refs = [refs.pop(0) for _ in range(n_w)]
    rw_ref = refs.pop(0) if epi == "rowscale" else None
    o_ref = refs.pop(0)
    m = pl.program_id(1)
    k = pl.program_id(2)

    def finalize(parts):
        if epi == "swiglu":
            o_ref[...] = (_silu(parts[0]) * parts[1]).astype(o_ref.dtype)
        else:
            o_ref[...] = (parts[0] * rw_ref[...]).astype(o_ref.dtype)

    @pl.when(m < nu_ref[0])
    def _():
        _mm_body(a_ref, w_refs, o_ref, refs, nk, finalize)

    @pl.when(jnp.logical_and(m >= nu_ref[0], k == nk - 1))
    def _():
        o_ref[...] = jnp.zeros_like(o_ref)


def grouped_mm(a, ws, tile_e, n_used, *, tm, tn, tk, epi, out_dtype, row_w=None, name="gmm"):
    P, K = a.shape
    N = ws[0].shape[2]
    nm, nn, nk = P // tm, N // tn, K // tk
    assert P % tm == 0 and N % tn == 0 and K % tk == 0

    def mc(m, nu):
        return jnp.minimum(m, nu[0] - 1)

    in_specs = [pl.BlockSpec((tm, tk), lambda n, m, k, te, nu: (mc(m, nu), k))]
    in_specs += [pl.BlockSpec((None, tk, tn), lambda n, m, k, te, nu: (te[mc(m, nu)], k, n)) for _ in ws]
    args = [a, *ws]
    if epi == "rowscale":
        in_specs.append(pl.BlockSpec((tm, 1), lambda n, m, k, te, nu: (mc(m, nu), 0)))
        args.append(row_w)
    scratch = [pltpu.VMEM((tm, tn), F32) for _ in ws] if nk > 1 else []
    return pl.pallas_call(
        functools.partial(_grouped_mm_kernel, n_w=len(ws), nk=nk, epi=epi),
        grid_spec=pltpu.PrefetchScalarGridSpec(
            num_scalar_prefetch=2,
            grid=(nn, nm, nk),
            in_specs=in_specs,
            out_specs=pl.BlockSpec((tm, tn), lambda n, m, k, te, nu: (m, n)),
            scratch_shapes=scratch),
        out_shape=jax.ShapeDtypeStruct((P, N), out_dtype),
        compiler_params=_cp(("parallel", "arbitrary", "arbitrary")),
        name=name,
    )(tile_e, n_used, *args)


def _merge_kernel(h_ref, wg_ref, bg_ref, b0_ref, b1_ref, b2_ref, b3_ref, wb_ref, o_ref, acc_ref):
    j = pl.program_id(2)
    g = jnp.dot(h_ref[...], wg_ref[...], preferred_element_type=F32) + bg_ref[...]
    gate = _sigmoid(g)
    for idx, br in enumerate((b0_ref, b1_ref, b2_ref, b3_ref)):
        @pl.when(j == idx)
        def _(br=br, idx=idx):
            term = gate * jnp.dot(br[...], wb_ref[...], preferred_element_type=F32)
            if idx == 0:
                acc_ref[...] = term
            else:
                acc_ref[...] += term

    @pl.when(j == 3)
    def _():
        o_ref[...] = acc_ref[...].astype(o_ref.dtype)


def merge(h, branches, wg, bg, wb, *, rows, tm, tn):
    D = h.shape[1]
    BW = branches[0].shape[1]
    br_spec = pl.BlockSpec((tm, BW), lambda m, n, j: (m, 0))
    return pl.pallas_call(
        _merge_kernel,
        grid=(rows // tm, D // tn, 4),
        in_specs=[pl.BlockSpec((tm, D), lambda m, n, j: (m, 0)),
                  pl.BlockSpec((None, D, tn), lambda m, n, j: (j, 0, n)),
                  pl.BlockSpec((None, 1, tn), lambda m, n, j: (j, 0, n)),
                  br_spec, br_spec, br_spec, br_spec,
                  pl.BlockSpec((None, BW, tn), lambda m, n, j: (j, 0, n))],
        out_specs=pl.BlockSpec((tm, tn), lambda m, n, j: (m, n)),
        out_shape=jax.ShapeDtypeStruct((rows, D), BF16),
        scratch_shapes=[pltpu.VMEM((tm, tn), F32)],
        compiler_params=_cp(("parallel", "parallel", "arbitrary")),
        name="merge",
    )(h, wg, bg, *branches, wb)


def _seq_flags(geo, tb):
    row0 = pl.program_id(0) * tb
    lat = row0 < geo.BS
    rel = row0 - geo.BS
    start = jnp.where(lat, row0 % geo.S == 0, rel % geo.L == 0)
    end = jnp.where(lat, (row0 + tb) % geo.S == 0, (rel + tb) % geo.L == 0)
    return start, end


def _halo_specs(geo, tb, hb, width, col_block):
    r = tb // hb
    last = geo.N // hb - 1
    return [pl.BlockSpec((tb, width), lambda i: (i, col_block)),
            pl.BlockSpec((hb, width), lambda i: (jnp.maximum(i * r - 1, 0), col_block)),
            pl.BlockSpec((hb, width), lambda i: (jnp.minimum((i + 1) * r, last), col_block))]


def _conv_rows(u_ref, w_ref, base, r0, rows, K):
    acc = None
    for j in range(K):
        t = w_ref[j:j + 1, :] * u_ref[pl.ds(base + r0 + j, rows), :]
        acc = t if acc is None else acc + t
    return acc


_C_HB = 16


def _conformer_kernel(main_ref, prev_ref, next_ref, w_ref, b_ref, lg_ref, lb_ref, o_ref, u_ref, *, geo, tb):
    start, end = _seq_flags(geo, tb)

    def glu(blk):
        return blk[:, :C_W] * _sigmoid(blk[:, C_W:])

    u_ref[pl.ds(_C_HB, tb), :] = glu(main_ref[...])
    u_ref[pl.ds(0, _C_HB), :] = jnp.where(start, 0.0, glu(prev_ref[...]))
    u_ref[pl.ds(_C_HB + tb, _C_HB), :] = jnp.where(end, 0.0, glu(next_ref[...]))
    rc = 32
    for r0 in range(0, tb, rc):
        u = _conv_rows(u_ref, w_ref, _C_HB - C_K // 2, r0, rc, C_K) + b_ref[...]
        mu = jnp.mean(u, axis=-1, keepdims=True)
        xc = u - mu
        y = xc * lax.rsqrt(jnp.mean(xc * xc, axis=-1, keepdims=True) + EPS)
        y = y * lg_ref[...] + lb_ref[...]
        o_ref[pl.ds(r0, rc), :] = _silu(y).astype(o_ref.dtype)


def conformer(z, w, b, lg, lb, geo, tb):
    cb = _Z["c_glu"][1] // (2 * C_W)
    return pl.pallas_call(
        functools.partial(_conformer_kernel, geo=geo, tb=tb),
        grid=(geo.N // tb,),
        in_specs=_halo_specs(geo, tb, _C_HB, 2 * C_W, cb) + [
            pl.BlockSpec((C_K, C_W), lambda i: (0, 0)),
            pl.BlockSpec((1, C_W), lambda i: (0, 0)),
            pl.BlockSpec((1, C_W), lambda i: (0, 0)),
            pl.BlockSpec((1, C_W), lambda i: (0, 0))],
        out_specs=pl.BlockSpec((tb, C_W), lambda i: (i, 0)),
        out_shape=jax.ShapeDtypeStruct((geo.N, C_W), BF16),
        scratch_shapes=[pltpu.VMEM((tb + 2 * _C_HB, C_W), F32)],
        compiler_params=_cp(("parallel",)),
        name="conformer",
    )(z, z, z, w, b.reshape(1, C_W), lg.reshape(1, C_W), lb.reshape(1, C_W))


_M_HB = 8


def _mlstm_qk_kernel(main_ref, prev_ref, next_ref, w_ref, b_ref, q_ref, k_ref, u_ref, *, geo, tb):
    start, end = _seq_flags(geo, tb)
    u_ref[pl.ds(_M_HB, tb), :] = main_ref[...]
    u_ref[pl.ds(0, _M_HB), :] = jnp.where(start, 0.0, prev_ref[...])
    u_ref[pl.ds(_M_HB + tb, _M_HB), :] = jnp.where(end, 0.0, next_ref[...])
    rc = 32
    for r0 in range(0, tb, rc):
        y = _silu(_conv_rows(u_ref, w_ref, _M_HB - M_CONV // 2, r0, rc, M_CONV) + b_ref[...])
        q_ref[pl.ds(r0, rc), :] = y[:, :M_W].astype(q_ref.dtype)
        k_ref[pl.ds(r0, rc), :] = (y[:, M_W:] * (M_HD ** -0.5)).astype(k_ref.dtype)


def mlstm_qk(z, w, b, geo, tb):
    cb = _Z["m_qk"][1] // (2 * M_W)
    return pl.pallas_call(
        functools.partial(_mlstm_qk_kernel, geo=geo, tb=tb),
        grid=(geo.N // tb,),
        in_specs=_halo_specs(geo, tb, _M_HB, 2 * M_W, cb) + [
            pl.BlockSpec((M_CONV, 2 * M_W), lambda i: (0, 0)),
            pl.BlockSpec((1, 2 * M_W), lambda i: (0, 0))],
        out_specs=[pl.BlockSpec((tb, M_W), lambda i: (i, 0)), pl.BlockSpec((tb, M_W), lambda i: (i, 0))],
        out_shape=[jax.ShapeDtypeStruct((geo.N, M_W), BF16), jax.ShapeDtypeStruct((geo.N, M_W), BF16)],
        scratch_shapes=[pltpu.VMEM((tb + 2 * _M_HB, 2 * M_W), F32)],
        compiler_params=_cp(("parallel",)),
        name="mlstm_qk",
    )(z, z, z, w, b.reshape(1, 2 * M_W))


def _split3(x):
    hi = x.astype(BF16)
    r1 = x - hi.astype(F32)
    mid = r1.astype(BF16)
    lo = (r1 - mid.astype(F32)).astype(BF16)
    return hi, mid, lo


def _log_sigmoid(x):
    return jnp.minimum(x, 0.0) - jnp.log(1.0 + jnp.exp(-jnp.abs(x)))


def _mlstm_gate_kernel(g_ref, b_ref, bc_ref, cc_ref, *, tb):
    raw = g_ref[...] + b_ref[...]
    lf = _log_sigmoid(raw)
    t = lax.broadcasted_iota(jnp.int32, (tb, tb), 0)
    s = lax.broadcasted_iota(jnp.int32, (tb, tb), 1)
    same = (t // M_CHUNK) == (s // M_CHUNK)
    lower = jnp.where(jnp.logical_and(same, s <= t), 1.0, 0.0).astype(BF16)
    upper = jnp.where(jnp.logical_and(same, s >= t), 1.0, 0.0).astype(BF16)
    hi, mid, lo = _split3(lf)

    def csum(tri):
        d = lambda p: jnp.dot(tri, p, preferred_element_type=F32)
        return (d(lo) + d(mid)) + d(hi)

    lane = lax.broadcasted_iota(jnp.int32, (tb, LANE), 1)
    b_all = jnp.where(lane < 2 * M_HEADS, csum(lower), csum(upper))
    i_sh = pltpu.roll(raw, M_HEADS, axis=1)
    bc_ref[...] = b_all
    cc_ref[...] = i_sh - b_all


def mlstm_gates(z, gate_b, geo, tb):
    cb = _Z["m_g"][1] // LANE
    gb = jnp.zeros((1, LANE), F32).at[0, :4 * M_HEADS].set(gate_b)
    spec = pl.BlockSpec((tb, LANE), lambda i: (i, 0))
    return pl.pallas_call(
        functools.partial(_mlstm_gate_kernel, tb=tb),
        grid=(geo.N // tb,),
        in_specs=[pl.BlockSpec((tb, LANE), lambda i: (i, cb)), pl.BlockSpec((1, LANE), lambda i: (0, 0))],
        out_specs=[spec, spec],
        out_shape=[jax.ShapeDtypeStruct((geo.N, LANE), F32)] * 2,
        compiler_params=_cp(("parallel",)),
        name="mlstm_gates",
    )(z, gb)


def _gate_lane(d, h):
    return 4 + 8 * d + h


def _mlstm_scan_kernel(*refs, tb):
    (qf, kf, vf, bcf, ccf, crf, qb, kb, vb, bcb, ccb, crb, hf_ref, hb_ref, ct_ref, m_ref) = refs
    i = pl.program_id(1)

    @pl.when(i == 0)
    def _():
        ct_ref[...] = jnp.zeros_like(ct_ref)
        m_ref[...] = jnp.zeros_like(m_ref)

    nc = tb // M_CHUNK
    t_i = lax.broadcasted_iota(jnp.int32, (M_CHUNK, M_CHUNK), 0)
    s_i = lax.broadcasted_iota(jnp.int32, (M_CHUNK, M_CHUNK), 1)
    e0 = jnp.where(lax.broadcasted_iota(jnp.int32, (M_CHUNK, M_HD), 1) == 0, 1.0, 0.0)
    dirs = ((0, qf, kf, vf, bcf, ccf, crf, hf_ref, s_i <= t_i), (1, qb, kb, vb, bcb, ccb, crb, hb_ref, s_i >= t_i))
    for d, q_ref, k_ref, v_ref, bc_ref, cc_ref, cr_ref, h_ref, mask in dirs:
        chunks = range(nc) if d == 0 else range(nc - 1, -1, -1)
        last = M_CHUNK - 1 if d == 0 else 0
        for h in range(M_HEADS):
            ln = _gate_lane(d, h)
            sid = d * M_HEADS + h
            m_prev = m_ref[sid, 0:1, 0:1]
            ct = ct_ref[sid]
            for c in chunks:
                rows = pl.ds(c * M_CHUNK, M_CHUNK)
                lanes = slice(h * M_HD, (h + 1) * M_HD)
                qc = q_ref[rows, lanes]
                kc = k_ref[rows, lanes]
                vc = v_ref[rows, lanes]
                col_r = cr_ref[c, sid:sid + 1, :]
                col_c = cc_ref[rows, ln:ln + 1]
                b_c = bc_ref[rows, ln:ln + 1]
                cm = jnp.max(jnp.where(mask, col_r, -jnp.inf), axis=-1, keepdims=True)
                mt = jnp.maximum(m_prev, cm)
                s = lax.dot_general(qc, kc, (((1,), (1,)), ((), ())), preferred_element_type=F32)
                w = jnp.where(mask, jnp.exp(col_r - mt), 0.0) * s
                v_aug = jnp.concatenate([vc, e0], axis=1)
                tot = jnp.exp(m_prev - mt) * jnp.dot(qc, ct.astype(BF16), preferred_element_type=F32)
                tot = tot + jnp.dot(w.astype(BF16), v_aug.astype(BF16), preferred_element_type=F32)
                den = jnp.maximum(jnp.abs(tot[:, M_HD:M_HD + 1]), jnp.exp(-b_c - mt))
                h_ref[rows, lanes] = tot[:, :M_HD] / den
                m_end = mt[last:last + 1, :]
                wg = jnp.exp(col_c - m_end)
                upd = lax.dot_general(kc, (wg * v_aug).astype(BF16), (((0,), (0,)), ((), ())),
                                      preferred_element_type=F32)
                ct = jnp.exp(m_prev - m_end) * ct + upd
                m_prev = b_c[last:last + 1, :] + m_end
            ct_ref[sid] = ct
            m_ref[sid] = jnp.broadcast_to(m_prev, m_ref.shape[1:])


def mlstm_scan(q, k, z, bc, cc, cr, geo, tb):
    B, S, L = geo.B, geo.S, geo.L
    nctx, nlat = L // tb, S // tb
    nblk = nctx + nlat
    vcb = _Z["m_v"][1] // M_W

    def blk(b, i, d):
        ctx_i = i if d == 0 else nctx - 1 - i
        lat_i = i - nctx if d == 0 else nlat - 1 - (i - nctx)
        return jnp.where(i < nctx, (B * S) // tb + b * nctx + ctx_i, b * nlat + lat_i)

    def specs(d):
        return [pl.BlockSpec((tb, M_W), lambda b, i: (blk(b, i, d), 0)),
                pl.BlockSpec((tb, M_W), lambda b, i: (blk(b, i, d), 0)),
                pl.BlockSpec((tb, M_W), lambda b, i: (blk(b, i, d), vcb)),
                pl.BlockSpec((tb, LANE), lambda b, i: (blk(b, i, d), 0)),
                pl.BlockSpec((tb, LANE), lambda b, i: (blk(b, i, d), 0)),
                pl.BlockSpec((tb // M_CHUNK, 2 * M_HEADS, M_CHUNK), lambda b, i: (blk(b, i, d), 0, 0))]

    out_f = pl.BlockSpec((tb, M_W), lambda b, i: (blk(b, i, 0), 0))
    out_b = pl.BlockSpec((tb, M_W), lambda b, i: (blk(b, i, 1), 0))
    return pl.pallas_call(
        functools.partial(_mlstm_scan_kernel, tb=tb),
        grid=(B, nblk),
        in_specs=specs(0) + specs(1),
        out_specs=[out_f, out_b],
        out_shape=[jax.ShapeDtypeStruct((geo.N, M_W), F32)] * 2,
        scratch_shapes=[pltpu.VMEM((2 * M_HEADS, M_HD, 2 * M_HD), F32), pltpu.VMEM((2 * M_HEADS, 8, LANE), F32)],
        compiler_params=_cp(("parallel", "arbitrary")),
        name="mlstm_scan",
    )(q, k, z, bc, cc, cr, q, k, z, bc, cc, cr)


def _mlstm_out_kernel(hf_ref, hb_ref, o_ref, g_ref, a_ref):
    hsum = hf_ref[...] + hb_ref[...]
    gate = _sigmoid(o_ref[...])
    for h in range(M_HEADS):
        lanes = slice(h * M_HD, (h + 1) * M_HD)
        x = hsum[:, lanes]
        y = x * lax.rsqrt(jnp.mean(x * x, axis=-1, keepdims=True) + EPS) * g_ref[:, lanes]
        a_ref[:, lanes] = (y * gate[:, lanes]).astype(a_ref.dtype)


def mlstm_out(hf, hb, z, g, rows, tb):
    ocb = _Z["m_o"][1] // M_W
    spec = pl.BlockSpec((tb, M_W), lambda i: (i, 0))
    return pl.pallas_call(
        _mlstm_out_kernel,
        grid=(rows // tb,),
        in_specs=[spec, spec, pl.BlockSpec((tb, M_W), lambda i: (i, ocb)), pl.BlockSpec((1, M_W), lambda i: (0, 0))],
        out_specs=spec,
        out_shape=jax.ShapeDtypeStruct((rows, M_W), BF16),
        compiler_params=_cp(("parallel",)),
        name="mlstm_out",
    )(hf, hb, z, g.reshape(1, M_W))


def _rope_tables(S, d):
    r = d // 4
    t = np.arange(S)
    freqs = ROPE_THETA ** (-np.arange(r, dtype=np.float32) / r)
    rows = (t // GRID_W).astype(np.float32)[:, None] * freqs
    cols = (t % GRID_W).astype(np.float32)[:, None] * freqs
    rows, cols = jnp.asarray(rows, F32), jnp.asarray(cols, F32)
    cos = jnp.concatenate([jnp.cos(rows), jnp.cos(rows), jnp.cos(cols), jnp.cos(cols)], axis=1)
    sin = jnp.concatenate([-jnp.sin(rows), jnp.sin(rows), -jnp.sin(cols), jnp.sin(cols)], axis=1)
    rep = LANE // d
    return jnp.tile(cos, (1, rep)), jnp.tile(sin, (1, rep))


def _norm_rope(x, g, cos, sin, is_lat, d):
    r = d // 4
    if d == LANE:
        ms = jnp.mean(x * x, axis=-1, keepdims=True)
    else:
        parts = [jnp.broadcast_to(jnp.mean(x[:, o:o + d] * x[:, o:o + d], axis=-1, keepdims=True), (x.shape[0], d))
                 for o in range(0, LANE, d)]
        ms = jnp.concatenate(parts, axis=1)
    y = x * lax.rsqrt(ms + EPS) * g
    lane = lax.broadcasted_iota(jnp.int32, y.shape, 1)
    first = (lane // r) % 2 == 0
    partner = jnp.where(first, pltpu.roll(y, LANE - r, axis=1), pltpu.roll(y, r, axis=1))
    return jnp.where(is_lat, y * cos + partner * sin, y)


def _attn_prep_kernel(gq_ref, gk_ref, gv_ref, xq_ref, xk_ref, xv_ref, gqg_ref, gkg_ref, xqg_ref, xkg_ref,
                      cg_ref, sg_ref, cx_ref, sx_ref, oq_ref, ok_ref, ov_ref, oxq_ref, oxk_ref, oxv_ref, *, geo, tb):
    is_lat = pl.program_id(0) * tb < geo.BS
    cg, sg, cx, sx = cg_ref[...], sg_ref[...], cx_ref[...], sx_ref[...]
    for h in range(G_HEADS):
        ls = slice(h * LANE, (h + 1) * LANE)
        oq_ref[:, ls] = _norm_rope(gq_ref[:, ls], gqg_ref[...], cg, sg, is_lat, G_HD).astype(BF16)
    for h in range(G_KV):
        ls = slice(h * LANE, (h + 1) * LANE)
        ok_ref[:, ls] = _norm_rope(gk_ref[:, ls], gkg_ref[...], cg, sg, is_lat, G_HD).astype(BF16)
    ov_ref[...] = gv_ref[...].astype(BF16)
    lane = lax.broadcasted_iota(jnp.int32, (tb, LANE), 1)
    for h in range(X_HEADS):
        ls = slice(h * LANE, (h + 1) * LANE)
        q = _norm_rope(xq_ref[:, ls], xqg_ref[...], cx, sx, is_lat, X_HD)
        oxq_ref[:, 2 * h * LANE:(2 * h + 1) * LANE] = jnp.where(lane < X_HD, q, 0.0).astype(BF16)
        oxq_ref[:, (2 * h + 1) * LANE:(2 * h + 2) * LANE] = jnp.where(lane >= X_HD, q, 0.0).astype(BF16)
        oxk_ref[:, ls] = _norm_rope(xk_ref[:, ls], xkg_ref[...], cx, sx, is_lat, X_HD).astype(BF16)
    oxv_ref[...] = xv_ref[...].astype(BF16)


def attn_prep(z, gq_g, gk_g, xq_g, xk_g, geo, tb):
    N = geo.N
    cg, sg = _rope_tables(geo.S, G_HD)
    cx, sx = _rope_tables(geo.S, X_HD)
    nlat_t = geo.S // tb

    def zs(name, width):
        cb = _Z[name][1] // width
        return pl.BlockSpec((tb, width), lambda i: (i, cb))

    tab = pl.BlockSpec((tb, LANE), lambda i: (i % nlat_t, 0))
    vec = pl.BlockSpec((1, LANE), lambda i: (0, 0))
    row = lambda w: pl.BlockSpec((tb, w), lambda i: (i, 0))
    rep = LANE // X_HD
    return pl.pallas_call(
        functools.partial(_attn_prep_kernel, geo=geo, tb=tb),
        grid=(N // tb,),
        in_specs=[zs("g_q", 512), zs("g_k", 256), zs("g_v", 256), zs("x_q", 512), zs("x_k", 512), zs("x_v", 512),
                  vec, vec, vec, vec, tab, tab, tab, tab],
        out_specs=[row(512), row(256), row(256), row(1024), row(512), row(512)],
        out_shape=[jax.ShapeDtypeStruct((N, w), BF16) for w in (512, 256, 256, 1024, 512, 512)],
        compiler_params=_cp(("parallel",)),
        name="attn_prep",
    )(z, z, z, z, z, z, gq_g.reshape(1, LANE), gk_g.reshape(1, LANE),
      jnp.tile(xq_g, rep).reshape(1, LANE), jnp.tile(xk_g, rep).reshape(1, LANE), cg, sg, cx, sx)


def _flash_kernel(*refs, scale, tk, n_lat, mode, lam_init):
    refs = list(refs)
    q_ref = refs.pop(0)
    if n_lat:
        kl_ref, vl_ref = refs.pop(0), refs.pop(0)
    kc_ref, vc_ref = refs.pop(0), refs.pop(0)
    if mode == "diff":
        lam_ref, sg_ref = refs.pop(0), refs.pop(0)
    o_ref = refs.pop(0)
    tq = q_ref.shape[0]
    outs = []
    for g in range(2):
        q = q_ref[:, g * LANE:(g + 1) * LANE]

        def chunk(k, v, carry, q=q):
            m, l, acc = carry
            s = lax.dot_general(q, k, (((1,), (1,)), ((), ())), preferred_element_type=F32) * scale
            m_new = jnp.maximum(m, jnp.max(s, axis=-1, keepdims=True))
            p = jnp.exp(s - m_new)
            alpha = jnp.exp(m - m_new)
            l = alpha * l + jnp.sum(p, axis=-1, keepdims=True)
            acc = alpha * acc + jnp.dot(p.astype(BF16), v, preferred_element_type=F32)
            return m_new, l, acc

        carry = (jnp.full((tq, 1), -jnp.inf, F32), jnp.zeros((tq, 1), F32), jnp.zeros((tq, LANE), F32))
        if n_lat:
            def body(j, carry):
                rows = pl.ds(pl.multiple_of(j * tk, tk), tk)
                return chunk(kl_ref[rows, :], vl_ref[rows, :], carry)

            carry = lax.fori_loop(0, n_lat, body, carry)
        m, l, acc = chunk(kc_ref[...], vc_ref[...], carry)
        outs.append(acc / l)
    if mode == "gqa":
        o_ref[:, :LANE] = outs[0].astype(o_ref.dtype)
        o_ref[:, LANE:] = outs[1].astype(o_ref.dtype)
    else:
        dl = lam_ref[...]
        lam = (jnp.exp(jnp.sum(dl[0:1] * dl[1:2], axis=-1, keepdims=True))
               - jnp.exp(jnp.sum(dl[2:3] * dl[3:4], axis=-1, keepdims=True)) + lam_init)
        x = outs[0] - lam * outs[1]
        y = x * lax.rsqrt(jnp.mean(x * x, axis=-1, keepdims=True) + EPS) * sg_ref[...]
        o_ref[...] = (y * (1.0 - lam_init)).astype(o_ref.dtype)


def attention(q, k, v, geo, *, n_kv, d, mode, ctx_queries, lam=None, sub_g=None, lam_init=0.0):
    B, S, L = geo.B, geo.S, geo.L
    ctx0 = (B * S) // L
    if ctx_queries:
        tq, nq, n_lat, tk = L, 1, 0, L
        qrow = lambda b, h, i: ctx0 + b
        orow = lambda b, h, i: b
    else:
        tq = _pick(S, (256, 128, 64))
        nq = S // tq
        tk = _pick(S, (512, 256, 128, 64))
        n_lat = S // tk
        qrow = lambda b, h, i: b * nq + i
        orow = qrow
    n_out = B * tq * nq
    in_specs = [pl.BlockSpec((tq, 2 * LANE), lambda b, h, i: (qrow(b, h, i), h))]
    args = [q]
    if n_lat:
        in_specs += [pl.BlockSpec((S, LANE), lambda b, h, i: (b, h)), pl.BlockSpec((S, LANE), lambda b, h, i: (b, h))]
        args += [k, v]
    in_specs += [pl.BlockSpec((L, LANE), lambda b, h, i: (ctx0 + b, h)),
                 pl.BlockSpec((L, LANE), lambda b, h, i: (ctx0 + b, h))]
    args += [k, v]
    if mode == "diff":
        in_specs += [pl.BlockSpec((4, X_HD), lambda b, h, i: (0, 0)), pl.BlockSpec((1, LANE), lambda b, h, i: (0, 0))]
        args += [lam, sub_g.reshape(1, LANE)]
        ow = LANE
    else:
        ow = 2 * LANE
    return pl.pallas_call(
        functools.partial(_flash_kernel, scale=d ** -0.5, tk=tk, n_lat=n_lat, mode=mode, lam_init=lam_init),
        grid=(B, n_kv, nq),
        in_specs=in_specs,
        out_specs=pl.BlockSpec((tq, ow), lambda b, h, i: (orow(b, h, i), h)),
        out_shape=jax.ShapeDtypeStruct((n_out, n_kv * ow), BF16),
        compiler_params=_cp(("parallel", "parallel", "arbitrary")),
        name="attn_" + mode + ("_ctx" if ctx_queries else ""),
    )(*args)


def _router_kernel(t_ref, r_ref, o_ref):
    t = t_ref[...]
    th, tm_, _ = _split3(t)
    rh, rm, _ = _split3(r_ref[...])
    d = lambda a, b: jnp.dot(a, b, preferred_element_type=F32)
    logits = (d(tm_, rh) + d(th, rm)) + d(th, rh)
    lane = lax.broadcasted_iota(jnp.int32, logits.shape, 1).astype(F32)
    logits = jnp.where(lane < N_EXPERTS, logits, -jnp.inf)
    v1 = jnp.max(logits, axis=-1, keepdims=True)
    i1 = jnp.min(jnp.where(logits == v1, lane, float(LANE)), axis=-1, keepdims=True)
    rest = jnp.where(lane == i1, -jnp.inf, logits)
    v2 = jnp.max(rest, axis=-1, keepdims=True)
    i2 = jnp.min(jnp.where(rest == v2, lane, float(LANE)), axis=-1, keepdims=True)
    e2 = jnp.exp(v2 - v1)
    den = 1.0 + e2
    out = jnp.where(lane == 0, i1, 0.0)
    out = jnp.where(lane == 1, i2, out)
    out = jnp.where(lane == 2, 1.0 / den, out)
    out = jnp.where(lane == 3, e2 / den, out)
    o_ref[...] = out


def router(t, r, tm):
    N, D = t.shape
    rp = jnp.zeros((D, LANE), F32).at[:, :N_EXPERTS].set(r)
    return pl.pallas_call(
        _router_kernel,
        grid=(N // tm,),
        in_specs=[pl.BlockSpec((tm, D), lambda m: (m, 0)), pl.BlockSpec((D, LANE), lambda m: (0, 0))],
        out_specs=pl.BlockSpec((tm, LANE), lambda m: (m, 0)),
        out_shape=jax.ShapeDtypeStruct((N, LANE), F32),
        compiler_params=_cp(("parallel",)),
        name="router",
    )(t, rp)


def _row_copy(src_hbm, dst_vmem, sem, src_row, dst_row):
    return pltpu.make_async_copy(src_hbm.at[pl.ds(src_row, 1), :], dst_vmem.at[pl.ds(dst_row, 1), :], sem)


def _gather_rows(src_hbm, idx_smem, dst_vmem, sem, n):
    def issue(r, c):
        _row_copy(src_hbm, dst_vmem, sem, idx_smem[0, 0, r], r).start()
        return c

    lax.fori_loop(0, n, issue, 0)

    def drain(r, c):
        _row_copy(src_hbm, dst_vmem, sem, 0, r).wait()
        return c

    lax.fori_loop(0, n, drain, 0)


def _load_idx(idx_vmem, idx_smem, sem):
    cp = pltpu.make_async_copy(idx_vmem, idx_smem, sem)
    cp.start()
    cp.wait()


def _gather_kernel(idx_ref, src_ref, o_ref, idx_smem, buf, sem_i, sem_r, *, tg):
    _load_idx(idx_ref, idx_smem, sem_i)
    _gather_rows(src_ref, idx_smem, buf, sem_r, tg)
    o_ref[...] = buf[...].astype(o_ref.dtype)


def gather_rows(src, idx, tg, out_dtype):
    P = idx.shape[0]
    D = src.shape[1]
    return pl.pallas_call(
        functools.partial(_gather_kernel, tg=tg),
        grid=(P // tg,),
        in_specs=[pl.BlockSpec((1, 1, tg), lambda i: (i, 0, 0)), pl.BlockSpec(memory_space=pl.ANY)],
        out_specs=pl.BlockSpec((tg, D), lambda i: (i, 0)),
        out_shape=jax.ShapeDtypeStruct((P, D), out_dtype),
        scratch_shapes=[pltpu.SMEM((1, 1, tg), jnp.int32), pltpu.VMEM((tg, D), src.dtype),
                        pltpu.SemaphoreType.DMA(()), pltpu.SemaphoreType.DMA(())],
        compiler_params=_cp(("arbitrary",)),
        name="moe_gather",
    )(idx.reshape(P // tg, 1, tg), src)


def _combine_kernel(p0_ref, p1_ref, ys_ref, x_ref, gate_ref, o_ref, s0, s1, b0, b1, sem_i, sem_r, *, tg):
    _load_idx(p0_ref, s0, sem_i)
    _load_idx(p1_ref, s1, sem_i)
    _gather_rows(ys_ref, s0, b0, sem_r, tg)
    _gather_rows(ys_ref, s1, b1, sem_r, tg)
    o_ref[...] = x_ref[...] + gate_ref[...] * (b0[...] + b1[...])


def moe_combine(ys, pos0, pos1, x, mod4, geo, gate_idx, rows, tg):
    D = geo.D
    idx_spec = pl.BlockSpec((1, 1, tg), lambda i: (i, 0, 0))
    return pl.pallas_call(
        functools.partial(_combine_kernel, tg=tg),
        grid=(rows // tg,),
        in_specs=[idx_spec, idx_spec, pl.BlockSpec(memory_space=pl.ANY),
                  pl.BlockSpec((tg, D), lambda i: (i, 0)),
                  pl.BlockSpec((None, None, 1, D), lambda i: (geo.group(i * tg), gate_idx, 0, 0))],
        out_specs=pl.BlockSpec((tg, D), lambda i: (i, 0)),
        out_shape=jax.ShapeDtypeStruct((rows, D), F32),
        scratch_shapes=[pltpu.SMEM((1, 1, tg), jnp.int32), pltpu.SMEM((1, 1, tg), jnp.int32),
                        pltpu.VMEM((tg, D), F32), pltpu.VMEM((tg, D), F32),
                        pltpu.SemaphoreType.DMA(()), pltpu.SemaphoreType.DMA(())],
        compiler_params=_cp(("arbitrary",)),
        name="moe_combine",
    )(pos0.reshape(rows // tg, 1, tg), pos1.reshape(rows // tg, 1, tg), ys, x, mod4)


def moe_ffn(h2, x, mod4, geo, gate_idx, r_w, w1, w3, w2, rows):
    D = geo.D
    E = w1.shape[0]
    F = w1.shape[2]
    tr = _pick(rows, (512, 256, 128, 64))
    route = router(h2, r_w, tr)
    e_flat = route[:, :TOP_K].astype(jnp.int32).reshape(-1)
    w_flat = route[:, TOP_K:2 * TOP_K].reshape(-1)
    n_assign = rows * TOP_K
    tm = _pick(rows, (512, 256, 128, 64))
    onehot = (e_flat[:, None] == jnp.arange(E, dtype=jnp.int32)[None, :]).astype(jnp.int32)
    csum = jnp.cumsum(onehot, axis=0)
    rank = jnp.sum(csum * onehot, axis=1) - 1
    counts = csum[-1]
    padded = (counts + tm - 1) // tm * tm
    pad_end = jnp.cumsum(padded)
    pad_start = pad_end - padded
    pos = pad_start[e_flat] + rank
    P = n_assign + E * tm
    n_tiles = P // tm
    tok_flat = jnp.repeat(jnp.arange(rows, dtype=jnp.int32), TOP_K)
    buf_tok = jnp.zeros((P,), jnp.int32).at[pos].set(tok_flat)
    buf_w = jnp.zeros((P,), F32).at[pos].set(w_flat)
    tile_e = jnp.minimum(jnp.searchsorted(pad_end, jnp.arange(n_tiles, dtype=jnp.int32) * tm, side="right"),
                         E - 1).astype(jnp.int32)
    n_used = (pad_end[-1] // tm).astype(jnp.int32).reshape(1)

    tg = _pick(rows, (256, 128, 64))
    xs = gather_rows(h2, buf_tok, tg, BF16)
    tn1 = _pick(F, (512, 256, 128))
    hb = grouped_mm(xs, [w1, w3], tile_e, n_used, tm=tm, tn=tn1, tk=D, epi="swiglu", out_dtype=BF16, name="moe_up")
    tn2 = _pick(D, (512, 256, 128))
    ys = grouped_mm(hb, [w2], tile_e, n_used, tm=tm, tn=tn2, tk=F, epi="rowscale", out_dtype=F32,
                    row_w=buf_w.reshape(P, 1), name="moe_down")
    pos2 = pos.reshape(rows, TOP_K)
    return moe_combine(ys, pos2[:, 0], pos2[:, 1], x, mod4, geo, gate_idx, rows, tg)


def _pack_w_in(w):
    D = w.shape[0]
    cols, at = [], 0
    for piece, off, width in sorted(_Z.values(), key=lambda v: v[1]):
        assert off == at
        lo, hi = _REF_OFF[piece], _REF_OFF[piece + 1]
        cols.append(w[:, lo:hi].astype(BF16))
        if width > hi - lo:
            cols.append(jnp.zeros((D, width - (hi - lo)), BF16))
        at += width
    assert at == Z_W
    return jnp.concatenate(cols, axis=1)


def kernel(x, c, ctx, c_ctx, ada_w, ada_b, norm1_g, norm2_g, w_in, mlstm_conv_w, mlstm_conv_b, mlstm_gate_b, mlstm_norm_g, conv_dw_w, conv_dw_b, conv_ln_g, conv_ln_b, gqa_q_norm_g, gqa_k_norm_g, diff_q_norm_g, diff_k_norm_g, diff_lambda, diff_subln_g, merge_gate_w, merge_gate_b, branch_w, out_w, ffn_w1, ffn_w3, ffn_w2, moe_router, moe_w1, moe_w3, moe_w2):
    B, S, D = x.shape
    L = ctx.shape[1]
    depth = ada_w.shape[0]
    geo = Geo(B, S, L, D)
    N, BS = geo.N, geo.BS
    assert S % GRID_W == 0 and S % M_CHUNK == 0 and L % M_CHUNK == 0 and BS % L == 0
    tb = _pick(math.gcd(S, L), (256, 128, 64))
    tm = _pick(math.gcd(S, B * L), (1024, 512, 256, 128))

    xt = jnp.concatenate([x.reshape(BS, D), ctx.reshape(B * L, D)], axis=0)
    cc = jnp.zeros((8, D), F32).at[:B].set(c).at[B].set(c_ctx)

    for l in range(depth):
        need_ctx = l < depth - 1
        rows = N if need_ctx else BS
        lam_init = 0.8 - 0.6 * math.exp(-0.3 * l)
        mod = adaln(cc, ada_w[l], ada_b[l])
        mod4 = mod[:B + 1].reshape(B + 1, 6, 1, D)

        h = norm_mod(xt, norm1_g[l], mod4, geo, 0, 1, N, BF16)
        z = dense_mm(h, [_pack_w_in(w_in[l])], rows=N, tm=tm, tn=_pick(Z_W, (640, 128)), tk=D, order="mn",
                     epi="none", out_dtype=F32, name="w_in")

        mq, mk = mlstm_qk(z, mlstm_conv_w[l], mlstm_conv_b[l], geo, tb)
        bc, cg = mlstm_gates(z, mlstm_gate_b[l], geo, tb)
        lanes = np.array([_gate_lane(d, hh) for d in range(2) for hh in range(M_HEADS)])
        cr = cg[:, lanes].reshape(N // M_CHUNK, M_CHUNK, 2 * M_HEADS).transpose(0, 2, 1)
        hf, hbw = mlstm_scan(mq, mk, z, bc, cg, cr, geo, tb)
        a_br = mlstm_out(hf, hbw, z, mlstm_norm_g[l], rows, tb)

        b_br = conformer(z, conv_dw_w[l], conv_dw_b[l], conv_ln_g[l], conv_ln_b[l], geo, tb)

        gq, gk, gv, xq, xk, xv = attn_prep(z, gqa_q_norm_g[l], gqa_k_norm_g[l], diff_q_norm_g[l], diff_k_norm_g[l],
                                           geo, tb)
        dkw = dict(lam=diff_lambda[l], sub_g=diff_subln_g[l], lam_init=lam_init)
        c_br = attention(gq, gk, gv, geo, n_kv=G_KV, d=G_HD, mode="gqa", ctx_queries=False)
        d_br = attention(xq, xk, xv, geo, n_kv=X_HEADS, d=X_HD, mode="diff", ctx_queries=False, **dkw)
        if need_ctx:
            c_ctx_o = attention(gq, gk, gv, geo, n_kv=G_KV, d=G_HD, mode="gqa", ctx_queries=True)
            d_ctx_o = attention(xq, xk, xv, geo, n_kv=X_HEADS, d=X_HD, mode="diff", ctx_queries=True, **dkw)
            c_br = jnp.concatenate([c_br, c_ctx_o], axis=0)
            d_br = jnp.concatenate([d_br, d_ctx_o], axis=0)

        tn = _pick(D, (512, 256, 128))
        y = merge(h, [a_br, b_br, c_br, d_br], merge_gate_w[l].astype(BF16), merge_gate_b[l].reshape(4, 1, D),
                  branch_w[l].astype(BF16), rows=rows, tm=tm, tn=tn)
        xt = dense_mm(y, [out_w[l].astype(BF16)], rows=rows, tm=tm, tn=tn, tk=D, order="mn", epi="resid",
                      out_dtype=F32, geo=geo, res=xt, mod4=mod4, gate_idx=2, name="out_proj")

        i = l // 2
        if l % 2 == 0:
            h2 = norm_mod(xt, norm2_g[l], mod4, geo, 3, 4, rows, BF16)
            F = ffn_w1.shape[2]
            tnf = _pick(F, (512, 256, 128))
            hb = dense_mm(h2, [ffn_w1[i].astype(BF16), ffn_w3[i].astype(BF16)], rows=rows, tm=tm, tn=tnf, tk=D,
                          order="mn", epi="swiglu", out_dtype=BF16, name="ffn_up")
            xt = dense_mm(hb, [ffn_w2[i].astype(BF16)], rows=rows, tm=tm, tn=_pick(D, (1024, 512, 256, 128)),
                          tk=_pick(F, (512, 256, 128)),
                          order="mn", epi="resid", out_dtype=F32, geo=geo, res=xt, mod4=mod4, gate_idx=5,
                          name="ffn_down")
        else:
            h2 = norm_mod(xt, norm2_g[l], mod4, geo, 3, 4, rows, F32)
            xt = moe_ffn(h2, xt, mod4, geo, 5, moe_router[i], moe_w1[i].astype(BF16), moe_w3[i].astype(BF16),
                         moe_w2[i].astype(BF16), rows)
    return xt[:BS].reshape(B, S, D)
```

```python
import functools
import math

import numpy as np
import jax
import jax.numpy as jnp
from jax import lax
from jax.experimental import pallas as pl
from jax.experimental.pallas import tpu as pltpu

F32 = jnp.float32
BF16 = jnp.bfloat16

GRID_W = 64
M_HEADS, M_HD, M_CHUNK, M_CONV = 4, 128, 64, 3
M_W = M_HEADS * M_HD
C_W, C_K = 512, 31
G_HEADS, G_KV, G_HD = 4, 2, 128
X_HEADS, X_HD = 4, 64
X_VD = 2 * X_HD
ROPE_THETA = 10000.0
N_EXPERTS, TOP_K = 8, 2
EPS = 1e-6
LANE = 128

_REF_SPLITS = (2 * M_W, M_W, M_W, 4 * M_HEADS, 2 * C_W, G_HEADS * G_HD, G_KV * G_HD, G_KV * G_HD,
               X_HEADS * 2 * X_HD, X_HEADS * 2 * X_HD, X_HEADS * X_VD)
_REF_OFF = np.concatenate([[0], np.cumsum(_REF_SPLITS)]).tolist()
_Z = {
    "m_qk": (0, 0, 1024), "c_glu": (4, 1024, 1024), "m_v": (1, 2048, 512), "m_o": (2, 2560, 512),
    "g_q": (5, 3072, 512), "x_q": (8, 3584, 512), "x_k": (9, 4096, 512), "x_v": (10, 4608, 512),
    "g_k": (6, 5120, 256), "g_v": (7, 5376, 256), "m_g": (3, 5632, 128),
}
Z_W = 5760

_VMEM_LIMIT = 56 * 1024 * 1024


def _cp(sem, vmem=_VMEM_LIMIT):
    return pltpu.CompilerParams(dimension_semantics=sem, vmem_limit_bytes=vmem)


def _pick(n, cands):
    for c in cands:
        if n % c == 0:
            return c
    raise ValueError(f"no tile for {n} in {cands}")


def _sigmoid(x):
    return 1.0 / (1.0 + jnp.exp(-x))


def _silu(x):
    return x * _sigmoid(x)


def _adaln_kernel(c_ref, w_ref, b_ref, o_ref):
    s = _silu(c_ref[...]).astype(BF16)
    o_ref[...] = jnp.dot(s, w_ref[...].astype(BF16), preferred_element_type=F32) + b_ref[...]


def adaln(cc, w, b):
    R, D = cc.shape
    N = w.shape[1]
    tn = _pick(N, (512, 256, 128))
    return pl.pallas_call(
        _adaln_kernel,
        grid=(N // tn,),
        in_specs=[pl.BlockSpec((R, D), lambda n: (0, 0)),
                  pl.BlockSpec((D, tn), lambda n: (0, n)),
                  pl.BlockSpec((1, tn), lambda n: (0, n))],
        out_specs=pl.BlockSpec((R, tn), lambda n: (0, n)),
        out_shape=jax.ShapeDtypeStruct((R, N), F32),
        compiler_params=_cp(("parallel",)),
        name="adaln",
    )(cc, w, b.reshape(1, N))


class Geo:
    def __init__(self, B, S, L, D):
        self.B, self.S, self.L, self.D = B, S, L, D
        self.BS = B * S
        self.N = B * S + B * L

    def group(self, row0):
        return jnp.minimum(row0 // self.S, self.B)


def _norm_mod_kernel(x_ref, g_ref, sh_ref, sc_ref, o_ref):
    x = x_ref[...]
    y = x * lax.rsqrt(jnp.mean(x * x, axis=-1, keepdims=True) + EPS)
    y = y * g_ref[...]
    o_ref[...] = (y * (1.0 + sc_ref[...]) + sh_ref[...]).astype(o_ref.dtype)


def norm_mod(x, g, mod4, geo, i_shift, i_scale, rows, out_dtype):
    D = geo.D
    tm = _pick(math.gcd(geo.S, geo.B * geo.L), (256, 128, 64))
    return pl.pallas_call(
        _norm_mod_kernel,
        grid=(rows // tm,),
        in_specs=[pl.BlockSpec((tm, D), lambda m: (m, 0)),
                  pl.BlockSpec((1, D), lambda m: (0, 0)),
                  pl.BlockSpec((None, None, 1, D), lambda m: (geo.group(m * tm), i_shift, 0, 0)),
                  pl.BlockSpec((None, None, 1, D), lambda m: (geo.group(m * tm), i_scale, 0, 0))],
        out_specs=pl.BlockSpec((tm, D), lambda m: (m, 0)),
        out_shape=jax.ShapeDtypeStruct((rows, D), out_dtype),
        compiler_params=_cp(("parallel",)),
        name="norm_mod",
    )(x, g.reshape(1, D), mod4, mod4)


def _mm_body(a_ref, w_refs, o_ref, acc_refs, nk, finalize):
    k = pl.program_id(2)
    a = a_ref[...]
    if a.dtype != BF16:
        a = a.astype(BF16)
    parts = [jnp.dot(a, w[...], preferred_element_type=F32) for w in w_refs]
    if nk == 1:
        finalize(parts)
        return

    @pl.when(k == 0)
    def _():
        for acc, p in zip(acc_refs, parts):
            acc[...] = p

    @pl.when(k > 0)
    def _():
        for acc, p in zip(acc_refs, parts):
            acc[...] += p

    @pl.when(k == nk - 1)
    def _():
        finalize([acc[...] for acc in acc_refs])


def _dense_mm_kernel(*refs, n_w, nk, epi):
    refs = list(refs)
    a_ref = refs.pop(0)
    w_refs = [refs.pop(0) for _ in range(n_w)]
    extra = []
    if epi == "resid":
        extra = [refs.pop(0), refs.pop(0)]
    o_ref = refs.pop(0)

    def finalize(parts):
        if epi == "none":
            o_ref[...] = parts[0].astype(o_ref.dtype)
        elif epi == "swiglu":
            o_ref[...] = (_silu(parts[0]) * parts[1]).astype(o_ref.dtype)
        else:
            o_ref[...] = (extra[0][...] + extra[1][...] * parts[0]).astype(o_ref.dtype)

    _mm_body(a_ref, w_refs, o_ref, refs, nk, finalize)


def dense_mm(a, ws, *, rows, tm, tn, tk, order, epi, out_dtype, geo=None, res=None, mod4=None, gate_idx=None, name="mm"):
    K = a.shape[1]
    N = ws[0].shape[1]
    nm, nn, nk = rows // tm, N // tn, K // tk
    assert rows % tm == 0 and N % tn == 0 and K % tk == 0
    if order == "mn":
        grid = (nm, nn, nk)
        mi = lambda g0, g1: g0
        ni = lambda g0, g1: g1
    else:
        grid = (nn, nm, nk)
        mi = lambda g0, g1: g1
        ni = lambda g0, g1: g0
    in_specs = [pl.BlockSpec((tm, tk), lambda g0, g1, k: (mi(g0, g1), k))]
    in_specs += [pl.BlockSpec((tk, tn), lambda g0, g1, k: (k, ni(g0, g1))) for _ in ws]
    args = [a, *ws]
    if epi == "resid":
        in_specs.append(pl.BlockSpec((tm, tn), lambda g0, g1, k: (mi(g0, g1), ni(g0, g1))))
        in_specs.append(pl.BlockSpec((None, None, 1, tn),
                                     lambda g0, g1, k: (geo.group(mi(g0, g1) * tm), gate_idx, 0, ni(g0, g1))))
        args += [res, mod4]
    scratch = [pltpu.VMEM((tm, tn), F32) for _ in ws] if nk > 1 else []
    return pl.pallas_call(
        functools.partial(_dense_mm_kernel, n_w=len(ws), nk=nk, epi=epi),
        grid=grid,
        in_specs=in_specs,
        out_specs=pl.BlockSpec((tm, tn), lambda g0, g1, k: (mi(g0, g1), ni(g0, g1))),
        out_shape=jax.ShapeDtypeStruct((rows, N), out_dtype),
        scratch_shapes=scratch,
        compiler_params=_cp(("parallel", "parallel", "arbitrary")),
        name=name,
    )(*args)


def _grouped_mm_kernel(te_ref, nu_ref, *refs, n_w, nk, epi):
    refs = list(refs)
    a_ref = refs.pop(0)
    w_refs = [refs.pop(0) for _ in range(n_w)]
    rw_ref = refs.pop(0) if epi == "rowscale" else None
    o_ref = refs.pop(0)
    m = pl.program_id(1)
    k = pl.program_id(2)

    def finalize(parts):
        if epi == "swiglu":
            o_ref[...] = (_silu(parts[0]) * parts[1]).astype(o_ref.dtype)
        else:
            o_ref[...] = (parts[0] * rw_ref[...]).astype(o_ref.dtype)

    @pl.when(m < nu_ref[0])
    def _():
        _mm_body(a_ref, w_refs, o_ref, refs, nk, finalize)

    @pl.when(jnp.logical_and(m >= nu_ref[0], k == nk - 1))
    def _():
        o_ref[...] = jnp.zeros_like(o_ref)


def grouped_mm(a, ws, tile_e, n_used, *, tm, tn, tk, epi, out_dtype, row_w=None, name="gmm"):
    P, K = a.shape
    N = ws[0].shape[2]
    nm, nn, nk = P // tm, N // tn, K // tk
    assert P % tm == 0 and N % tn == 0 and K % tk == 0

    def mc(m, nu):
        return jnp.minimum(m, nu[0] - 1)

    in_specs = [pl.BlockSpec((tm, tk), lambda n, m, k, te, nu: (mc(m, nu), k))]
    in_specs += [pl.BlockSpec((None, tk, tn), lambda n, m, k, te, nu: (te[mc(m, nu)], k, n)) for _ in ws]
    args = [a, *ws]
    if epi == "rowscale":
        in_specs.append(pl.BlockSpec((tm, 1), lambda n, m, k, te, nu: (mc(m, nu), 0)))
        args.append(row_w)
    scratch = [pltpu.VMEM((tm, tn), F32) for _ in ws] if nk > 1 else []
    return pl.pallas_call(
        functools.partial(_grouped_mm_kernel, n_w=len(ws), nk=nk, epi=epi),
        grid_spec=pltpu.PrefetchScalarGridSpec(
            num_scalar_prefetch=2,
            grid=(nn, nm, nk),
            in_specs=in_specs,
            out_specs=pl.BlockSpec((tm, tn), lambda n, m, k, te, nu: (m, n)),
            scratch_shapes=scratch),
        out_shape=jax.ShapeDtypeStruct((P, N), out_dtype),
        compiler_params=_cp(("parallel", "arbitrary", "arbitrary")),
        name=name,
    )(tile_e, n_used, *args)


def _merge_kernel(h_ref, wg_ref, bg_ref, b0_ref, b1_ref, b2_ref, b3_ref, wb_ref, o_ref, acc_ref):
    j = pl.program_id(2)
    g = jnp.dot(h_ref[...], wg_ref[...], preferred_element_type=F32) + bg_ref[...]
    gate = _sigmoid(g)
    for idx, br in enumerate((b0_ref, b1_ref, b2_ref, b3_ref)):
        @pl.when(j == idx)
        def _(br=br, idx=idx):
            term = gate * jnp.dot(br[...], wb_ref[...], preferred_element_type=F32)
            if idx == 0:
                acc_ref[...] = term
            else:
                acc_ref[...] += term

    @pl.when(j == 3)
    def _():
        o_ref[...] = acc_ref[...].astype(o_ref.dtype)


def merge(h, branches, wg, bg, wb, *, rows, tm, tn):
    D = h.shape[1]
    BW = branches[0].shape[1]
    br_spec = pl.BlockSpec((tm, BW), lambda m, n, j: (m, 0))
    return pl.pallas_call(
        _merge_kernel,
        grid=(rows // tm, D // tn, 4),
        in_specs=[pl.BlockSpec((tm, D), lambda m, n, j: (m, 0)),
                  pl.BlockSpec((None, D, tn), lambda m, n, j: (j, 0, n)),
                  pl.BlockSpec((None, 1, tn), lambda m, n, j: (j, 0, n)),
                  br_spec, br_spec, br_spec, br_spec,
                  pl.BlockSpec((None, BW, tn), lambda m, n, j: (j, 0, n))],
        out_specs=pl.BlockSpec((tm, tn), lambda m, n, j: (m, n)),
        out_shape=jax.ShapeDtypeStruct((rows, D), BF16),
        scratch_shapes=[pltpu.VMEM((tm, tn), F32)],
        compiler_params=_cp(("parallel", "parallel", "arbitrary")),
        name="merge",
    )(h, wg, bg, *branches, wb)


def _seq_flags(geo, tb):
    row0 = pl.program_id(0) * tb
    lat = row0 < geo.BS
    rel = row0 - geo.BS
    start = jnp.where(lat, row0 % geo.S == 0, rel % geo.L == 0)
    end = jnp.where(lat, (row0 + tb) % geo.S == 0, (rel + tb) % geo.L == 0)
    return start, end


def _halo_specs(geo, tb, hb, width, col_block):
    r = tb // hb
    last = geo.N // hb - 1
    return [pl.BlockSpec((tb, width), lambda i: (i, col_block)),
            pl.BlockSpec((hb, width), lambda i: (jnp.maximum(i * r - 1, 0), col_block)),
            pl.BlockSpec((hb, width), lambda i: (jnp.minimum((i + 1) * r, last), col_block))]


def _conv_rows(u_ref, w_ref, base, r0, rows, K):
    acc = None
    for j in range(K):
        t = w_ref[j:j + 1, :] * u_ref[pl.ds(base + r0 + j, rows), :]
        acc = t if acc is None else acc + t
    return acc


_C_HB = 16


def _conformer_kernel(main_ref, prev_ref, next_ref, w_ref, b_ref, lg_ref, lb_ref, o_ref, u_ref, *, geo, tb):
    start, end = _seq_flags(geo, tb)

    def glu(blk):
        return blk[:, :C_W] * _sigmoid(blk[:, C_W:])

    u_ref[pl.ds(_C_HB, tb), :] = glu(main_ref[...])
    u_ref[pl.ds(0, _C_HB), :] = jnp.where(start, 0.0, glu(prev_ref[...]))
    u_ref[pl.ds(_C_HB + tb, _C_HB), :] = jnp.where(end, 0.0, glu(next_ref[...]))
    rc = 32
    for r0 in range(0, tb, rc):
        u = _conv_rows(u_ref, w_ref, _C_HB - C_K // 2, r0, rc, C_K) + b_ref[...]
        mu = jnp.mean(u, axis=-1, keepdims=True)
        xc = u - mu
        y = xc * lax.rsqrt(jnp.mean(xc * xc, axis=-1, keepdims=True) + EPS)
        y = y * lg_ref[...] + lb_ref[...]
        o_ref[pl.ds(r0, rc), :] = _silu(y).astype(o_ref.dtype)


def conformer(z, w, b, lg, lb, geo, tb):
    cb = _Z["c_glu"][1] // (2 * C_W)
    return pl.pallas_call(
        functools.partial(_conformer_kernel, geo=geo, tb=tb),
        grid=(geo.N // tb,),
        in_specs=_halo_specs(geo, tb, _C_HB, 2 * C_W, cb) + [
            pl.BlockSpec((C_K, C_W), lambda i: (0, 0)),
            pl.BlockSpec((1, C_W), lambda i: (0, 0)),
            pl.BlockSpec((1, C_W), lambda i: (0, 0)),
            pl.BlockSpec((1, C_W), lambda i: (0, 0))],
        out_specs=pl.BlockSpec((tb, C_W), lambda i: (i, 0)),
        out_shape=jax.ShapeDtypeStruct((geo.N, C_W), BF16),
        scratch_shapes=[pltpu.VMEM((tb + 2 * _C_HB, C_W), F32)],
        compiler_params=_cp(("parallel",)),
        name="conformer",
    )(z, z, z, w, b.reshape(1, C_W), lg.reshape(1, C_W), lb.reshape(1, C_W))


_M_HB = 8


def _mlstm_qk_kernel(main_ref, prev_ref, next_ref, w_ref, b_ref, q_ref, k_ref, u_ref, *, geo, tb):
    start, end = _seq_flags(geo, tb)
    u_ref[pl.ds(_M_HB, tb), :] = main_ref[...]
    u_ref[pl.ds(0, _M_HB), :] = jnp.where(start, 0.0, prev_ref[...])
    u_ref[pl.ds(_M_HB + tb, _M_HB), :] = jnp.where(end, 0.0, next_ref[...])
    rc = 32
    for r0 in range(0, tb, rc):
        y = _silu(_conv_rows(u_ref, w_ref, _M_HB - M_CONV // 2, r0, rc, M_CONV) + b_ref[...])
        q_ref[pl.ds(r0, rc), :] = y[:, :M_W].astype(q_ref.dtype)
        k_ref[pl.ds(r0, rc), :] = (y[:, M_W:] * (M_HD ** -0.5)).astype(k_ref.dtype)


def mlstm_qk(z, w, b, geo, tb):
    cb = _Z["m_qk"][1] // (2 * M_W)
    return pl.pallas_call(
        functools.partial(_mlstm_qk_kernel, geo=geo, tb=tb),
        grid=(geo.N // tb,),
        in_specs=_halo_specs(geo, tb, _M_HB, 2 * M_W, cb) + [
            pl.BlockSpec((M_CONV, 2 * M_W), lambda i: (0, 0)),
            pl.BlockSpec((1, 2 * M_W), lambda i: (0, 0))],
        out_specs=[pl.BlockSpec((tb, M_W), lambda i: (i, 0)), pl.BlockSpec((tb, M_W), lambda i: (i, 0))],
        out_shape=[jax.ShapeDtypeStruct((geo.N, M_W), BF16), jax.ShapeDtypeStruct((geo.N, M_W), BF16)],
        scratch_shapes=[pltpu.VMEM((tb + 2 * _M_HB, 2 * M_W), F32)],
        compiler_params=_cp(("parallel",)),
        name="mlstm_qk",
    )(z, z, z, w, b.reshape(1, 2 * M_W))


def _split3(x):
    hi = x.astype(BF16)
    r1 = x - hi.astype(F32)
    mid = r1.astype(BF16)
    lo = (r1 - mid.astype(F32)).astype(BF16)
    return hi, mid, lo


def _log_sigmoid(x):
    return jnp.minimum(x, 0.0) - jnp.log(1.0 + jnp.exp(-jnp.abs(x)))


def _mlstm_gate_kernel(g_ref, b_ref, bc_ref, cc_ref, *, tb):
    raw = g_ref[...] + b_ref[...]
    lf = _log_sigmoid(raw)
    t = lax.broadcasted_iota(jnp.int32, (tb, tb), 0)
    s = lax.broadcasted_iota(jnp.int32, (tb, tb), 1)
    same = (t // M_CHUNK) == (s // M_CHUNK)
    lower = jnp.where(jnp.logical_and(same, s <= t), 1.0, 0.0).astype(BF16)
    upper = jnp.where(jnp.logical_and(same, s >= t), 1.0, 0.0).astype(BF16)
    hi, mid, lo = _split3(lf)

    def csum(tri):
        d = lambda p: jnp.dot(tri, p, preferred_element_type=F32)
        return (d(lo) + d(mid)) + d(hi)

    lane = lax.broadcasted_iota(jnp.int32, (tb, LANE), 1)
    b_all = jnp.where(lane < 2 * M_HEADS, csum(lower), csum(upper))
    i_sh = pltpu.roll(raw, M_HEADS, axis=1)
    bc_ref[...] = b_all
    cc_ref[...] = i_sh - b_all


def mlstm_gates(z, gate_b, geo, tb):
    cb = _Z["m_g"][1] // LANE
    gb = jnp.zeros((1, LANE), F32).at[0, :4 * M_HEADS].set(gate_b)
    spec = pl.BlockSpec((tb, LANE), lambda i: (i, 0))
    return pl.pallas_call(
        functools.partial(_mlstm_gate_kernel, tb=tb),
        grid=(geo.N // tb,),
        in_specs=[pl.BlockSpec((tb, LANE), lambda i: (i, cb)), pl.BlockSpec((1, LANE), lambda i: (0, 0))],
        out_specs=[spec, spec],
        out_shape=[jax.ShapeDtypeStruct((geo.N, LANE), F32)] * 2,
        compiler_params=_cp(("parallel",)),
        name="mlstm_gates",
    )(z, gb)


def _gate_lane(d, h):
    return 4 + 8 * d + h


def _mlstm_scan_kernel(*refs, tb):
    (qf, kf, vf, bcf, ccf, crf, qb, kb, vb, bcb, ccb, crb, hf_ref, hb_ref, ct_ref, m_ref) = refs
    i = pl.program_id(1)

    @pl.when(i == 0)
    def _():
        ct_ref[...] = jnp.zeros_like(ct_ref)
        m_ref[...] = jnp.zeros_like(m_ref)

    nc = tb // M_CHUNK
    t_i = lax.broadcasted_iota(jnp.int32, (M_CHUNK, M_CHUNK), 0)
    s_i = lax.broadcasted_iota(jnp.int32, (M_CHUNK, M_CHUNK), 1)
    e0 = jnp.where(lax.broadcasted_iota(jnp.int32, (M_CHUNK, M_HD), 1) == 0, 1.0, 0.0)
    dirs = ((0, qf, kf, vf, bcf, ccf, crf, hf_ref, s_i <= t_i), (1, qb, kb, vb, bcb, ccb, crb, hb_ref, s_i >= t_i))
    for d, q_ref, k_ref, v_ref, bc_ref, cc_ref, cr_ref, h_ref, mask in dirs:
        chunks = range(nc) if d == 0 else range(nc - 1, -1, -1)
        last = M_CHUNK - 1 if d == 0 else 0
        for h in range(M_HEADS):
            ln = _gate_lane(d, h)
            sid = d * M_HEADS + h
            m_prev = m_ref[sid, 0:1, 0:1]
            ct = ct_ref[sid]
            for c in chunks:
                rows = pl.ds(c * M_CHUNK, M_CHUNK)
                lanes = slice(h * M_HD, (h + 1) * M_HD)
                qc = q_ref[rows, lanes]
                kc = k_ref[rows, lanes]
                vc = v_ref[rows, lanes]
                col_r = cr_ref[c, sid:sid + 1, :]
                col_c = cc_ref[rows, ln:ln + 1]
                b_c = bc_ref[rows, ln:ln + 1]
                cm = jnp.max(jnp.where(mask, col_r, -jnp.inf), axis=-1, keepdims=True)
                mt = jnp.maximum(m_prev, cm)
                s = lax.dot_general(qc, kc, (((1,), (1,)), ((), ())), preferred_element_type=F32)
                w = jnp.where(mask, jnp.exp(col_r - mt), 0.0) * s
                v_aug = jnp.concatenate([vc, e0], axis=1)
                tot = jnp.exp(m_prev - mt) * jnp.dot(qc, ct.astype(BF16), preferred_element_type=F32)
                tot = tot + jnp.dot(w.astype(BF16), v_aug.astype(BF16), preferred_element_type=F32)
                den = jnp.maximum(jnp.abs(tot[:, M_HD:M_HD + 1]), jnp.exp(-b_c - mt))
                h_ref[rows, lanes] = tot[:, :M_HD] / den
                m_end = mt[last:last + 1, :]
                wg = jnp.exp(col_c - m_end)
                upd = lax.dot_general(kc, (wg * v_aug).astype(BF16), (((0,), (0,)), ((), ())),
                                      preferred_element_type=F32)
                ct = jnp.exp(m_prev - m_end) * ct + upd
                m_prev = b_c[last:last + 1, :] + m_end
            ct_ref[sid] = ct
            m_ref[sid] = jnp.broadcast_to(m_prev, m_ref.shape[1:])


def mlstm_scan(q, k, z, bc, cc, cr, geo, tb):
    B, S, L = geo.B, geo.S, geo.L
    nctx, nlat = L // tb, S // tb
    nblk = nctx + nlat
    vcb = _Z["m_v"][1] // M_W

    def blk(b, i, d):
        ctx_i = i if d == 0 else nctx - 1 - i
        lat_i = i - nctx if d == 0 else nlat - 1 - (i - nctx)
        return jnp.where(i < nctx, (B * S) // tb + b * nctx + ctx_i, b * nlat + lat_i)

    def specs(d):
        return [pl.BlockSpec((tb, M_W), lambda b, i: (blk(b, i, d), 0)),
                pl.BlockSpec((tb, M_W), lambda b, i: (blk(b, i, d), 0)),
                pl.BlockSpec((tb, M_W), lambda b, i: (blk(b, i, d), vcb)),
                pl.BlockSpec((tb, LANE), lambda b, i: (blk(b, i, d), 0)),
                pl.BlockSpec((tb, LANE), lambda b, i: (blk(b, i, d), 0)),
                pl.BlockSpec((tb // M_CHUNK, 2 * M_HEADS, M_CHUNK), lambda b, i: (blk(b, i, d), 0, 0))]

    out_f = pl.BlockSpec((tb, M_W), lambda b, i: (blk(b, i, 0), 0))
    out_b = pl.BlockSpec((tb, M_W), lambda b, i: (blk(b, i, 1), 0))
    return pl.pallas_call(
        functools.partial(_mlstm_scan_kernel, tb=tb),
        grid=(B, nblk),
        in_specs=specs(0) + specs(1),
        out_specs=[out_f, out_b],
        out_shape=[jax.ShapeDtypeStruct((geo.N, M_W), F32)] * 2,
        scratch_shapes=[pltpu.VMEM((2 * M_HEADS, M_HD, 2 * M_HD), F32), pltpu.VMEM((2 * M_HEADS, 8, LANE), F32)],
        compiler_params=_cp(("parallel", "arbitrary")),
        name="mlstm_scan",
    )(q, k, z, bc, cc, cr, q, k, z, bc, cc, cr)


def _mlstm_out_kernel(hf_ref, hb_ref, o_ref, g_ref, a_ref):
    hsum = hf_ref[...] + hb_ref[...]
    gate = _sigmoid(o_ref[...])
    for h in range(M_HEADS):
        lanes = slice(h * M_HD, (h + 1) * M_HD)
        x = hsum[:, lanes]
        y = x * lax.rsqrt(jnp.mean(x * x, axis=-1, keepdims=True) + EPS) * g_ref[:, lanes]
        a_ref[:, lanes] = (y * gate[:, lanes]).astype(a_ref.dtype)


def mlstm_out(hf, hb, z, g, rows, tb):
    ocb = _Z["m_o"][1] // M_W
    spec = pl.BlockSpec((tb, M_W), lambda i: (i, 0))
    return pl.pallas_call(
        _mlstm_out_kernel,
        grid=(rows // tb,),
        in_specs=[spec, spec, pl.BlockSpec((tb, M_W), lambda i: (i, ocb)), pl.BlockSpec((1, M_W), lambda i: (0, 0))],
        out_specs=spec,
        out_shape=jax.ShapeDtypeStruct((rows, M_W), BF16),
        compiler_params=_cp(("parallel",)),
        name="mlstm_out",
    )(hf, hb, z, g.reshape(1, M_W))


def _rope_tables(S, d):
    r = d // 4
    t = np.arange(S)
    freqs = ROPE_THETA ** (-np.arange(r, dtype=np.float32) / r)
    rows = (t // GRID_W).astype(np.float32)[:, None] * freqs
    cols = (t % GRID_W).astype(np.float32)[:, None] * freqs
    rows, cols = jnp.asarray(rows, F32), jnp.asarray(cols, F32)
    cos = jnp.concatenate([jnp.cos(rows), jnp.cos(rows), jnp.cos(cols), jnp.cos(cols)], axis=1)
    sin = jnp.concatenate([-jnp.sin(rows), jnp.sin(rows), -jnp.sin(cols), jnp.sin(cols)], axis=1)
    rep = LANE // d
    return jnp.tile(cos, (1, rep)), jnp.tile(sin, (1, rep))


def _norm_rope(x, g, cos, sin, is_lat, d):
    r = d // 4
    if d == LANE:
        ms = jnp.mean(x * x, axis=-1, keepdims=True)
    else:
        parts = [jnp.broadcast_to(jnp.mean(x[:, o:o + d] * x[:, o:o + d], axis=-1, keepdims=True), (x.shape[0], d))
                 for o in range(0, LANE, d)]
        ms = jnp.concatenate(parts, axis=1)
    y = x * lax.rsqrt(ms + EPS) * g
    lane = lax.broadcasted_iota(jnp.int32, y.shape, 1)
    first = (lane // r) % 2 == 0
    partner = jnp.where(first, pltpu.roll(y, LANE - r, axis=1), pltpu.roll(y, r, axis=1))
    return jnp.where(is_lat, y * cos + partner * sin, y)


def _attn_prep_kernel(gq_ref, gk_ref, gv_ref, xq_ref, xk_ref, xv_ref, gqg_ref, gkg_ref, xqg_ref, xkg_ref,
                      cg_ref, sg_ref, cx_ref, sx_ref, oq_ref, ok_ref, ov_ref, oxq_ref, oxk_ref, oxv_ref, *, geo, tb):
    is_lat = pl.program_id(0) * tb < geo.BS
    cg, sg, cx, sx = cg_ref[...], sg_ref[...], cx_ref[...], sx_ref[...]
    for h in range(G_HEADS):
        ls = slice(h * LANE, (h + 1) * LANE)
        oq_ref[:, ls] = _norm_rope(gq_ref[:, ls], gqg_ref[...], cg, sg, is_lat, G_HD).astype(BF16)
    for h in range(G_KV):
        ls = slice(h * LANE, (h + 1) * LANE)
        ok_ref[:, ls] = _norm_rope(gk_ref[:, ls], gkg_ref[...], cg, sg, is_lat, G_HD).astype(BF16)
    ov_ref[...] = gv_ref[...].astype(BF16)
    lane = lax.broadcasted_iota(jnp.int32, (tb, LANE), 1)
    for h in range(X_HEADS):
        ls = slice(h * LANE, (h + 1) * LANE)
        q = _norm_rope(xq_ref[:, ls], xqg_ref[...], cx, sx, is_lat, X_HD)
        oxq_ref[:, 2 * h * LANE:(2 * h + 1) * LANE] = jnp.where(lane < X_HD, q, 0.0).astype(BF16)
        oxq_ref[:, (2 * h + 1) * LANE:(2 * h + 2) * LANE] = jnp.where(lane >= X_HD, q, 0.0).astype(BF16)
        oxk_ref[:, ls] = _norm_rope(xk_ref[:, ls], xkg_ref[...], cx, sx, is_lat, X_HD).astype(BF16)
    oxv_ref[...] = xv_ref[...].astype(BF16)


def attn_prep(z, gq_g, gk_g, xq_g, xk_g, geo, tb):
    N = geo.N
    cg, sg = _rope_tables(geo.S, G_HD)
    cx, sx = _rope_tables(geo.S, X_HD)
    nlat_t = geo.S // tb

    def zs(name, width):
        cb = _Z[name][1] // width
        return pl.BlockSpec((tb, width), lambda i: (i, cb))

    tab = pl.BlockSpec((tb, LANE), lambda i: (i % nlat_t, 0))
    vec = pl.BlockSpec((1, LANE), lambda i: (0, 0))
    row = lambda w: pl.BlockSpec((tb, w), lambda i: (i, 0))
    rep = LANE // X_HD
    return pl.pallas_call(
        functools.partial(_attn_prep_kernel, geo=geo, tb=tb),
        grid=(N // tb,),
        in_specs=[zs("g_q", 512), zs("g_k", 256), zs("g_v", 256), zs("x_q", 512), zs("x_k", 512), zs("x_v", 512),
                  vec, vec, vec, vec, tab, tab, tab, tab],
        out_specs=[row(512), row(256), row(256), row(1024), row(512), row(512)],
        out_shape=[jax.ShapeDtypeStruct((N, w), BF16) for w in (512, 256, 256, 1024, 512, 512)],
        compiler_params=_cp(("parallel",)),
        name="attn_prep",
    )(z, z, z, z, z, z, gq_g.reshape(1, LANE), gk_g.reshape(1, LANE),
      jnp.tile(xq_g, rep).reshape(1, LANE), jnp.tile(xk_g, rep).reshape(1, LANE), cg, sg, cx, sx)


def _flash_kernel(*refs, scale, tk, n_lat, mode, lam_init):
    refs = list(refs)
    q_ref = refs.pop(0)
    if n_lat:
        kl_ref, vl_ref = refs.pop(0), refs.pop(0)
    kc_ref, vc_ref = refs.pop(0), refs.pop(0)
    if mode == "diff":
        lam_ref, sg_ref = refs.pop(0), refs.pop(0)
    o_ref = refs.pop(0)
    tq = q_ref.shape[0]
    q = jnp.concatenate([q_ref[:, :LANE], q_ref[:, LANE:]], axis=0)
    c_exp = scale * math.log2(math.e)

    def chunk(k, v, carry):
        m, l, acc = carry
        s = lax.dot_general(q, k, (((1,), (1,)), ((), ())), preferred_element_type=F32)
        m_new = jnp.maximum(m, jnp.max(s, axis=-1, keepdims=True))
        p = jnp.exp2((s - m_new) * c_exp)
        alpha = jnp.exp2((m - m_new) * c_exp)
        l = alpha * l + jnp.sum(p, axis=-1, keepdims=True)
        acc = alpha * acc + jnp.dot(p.astype(BF16), v, preferred_element_type=F32)
        return m_new, l, acc

    carry = (jnp.full((2 * tq, 1), -jnp.inf, F32), jnp.zeros((2 * tq, 1), F32), jnp.zeros((2 * tq, LANE), F32))
    if n_lat:
        def body(j, carry):
            rows = pl.ds(pl.multiple_of(j * tk, tk), tk)
            return chunk(kl_ref[rows, :], vl_ref[rows, :], carry)

        carry = lax.fori_loop(0, n_lat, body, carry, unroll=8 if n_lat % 8 == 0 else 1)
    _, l, acc = chunk(kc_ref[...], vc_ref[...], carry)
    o = acc / l
    outs = [o[:tq], o[tq:]]
    if mode == "gqa":
        o_ref[:, :LANE] = outs[0].astype(o_ref.dtype)
        o_ref[:, LANE:] = outs[1].astype(o_ref.dtype)
    else:
        dl = lam_ref[...]
        lam = (jnp.exp(jnp.sum(dl[0:1] * dl[1:2], axis=-1, keepdims=True))
               - jnp.exp(jnp.sum(dl[2:3] * dl[3:4], axis=-1, keepdims=True)) + lam_init)
        x = outs[0] - lam * outs[1]
        y = x * lax.rsqrt(jnp.mean(x * x, axis=-1, keepdims=True) + EPS) * sg_ref[...]
        o_ref[...] = (y * (1.0 - lam_init)).astype(o_ref.dtype)


def attention(q, k, v, geo, *, n_kv, d, mode, ctx_queries, lam=None, sub_g=None, lam_init=0.0):
    B, S, L = geo.B, geo.S, geo.L
    ctx0 = (B * S) // L
    if ctx_queries:
        tq, nq, n_lat, tk = L, 1, 0, L
        qrow = lambda b, h, i: ctx0 + b
        orow = lambda b, h, i: b
    else:
        tq = _pick(S, (256, 128, 64))
        nq = S // tq
        tk = _pick(S, (1024, 512, 256, 128, 64))
        n_lat = S // tk
        qrow = lambda b, h, i: b * nq + i
        orow = qrow
    n_out = B * tq * nq
    in_specs = [pl.BlockSpec((tq, 2 * LANE), lambda b, h, i: (qrow(b, h, i), h))]
    args = [q]
    if n_lat:
        in_specs += [pl.BlockSpec((S, LANE), lambda b, h, i: (b, h)), pl.BlockSpec((S, LANE), lambda b, h, i: (b, h))]
        args += [k, v]
    in_specs += [pl.BlockSpec((L, LANE), lambda b, h, i: (ctx0 + b, h)),
                 pl.BlockSpec((L, LANE), lambda b, h, i: (ctx0 + b, h))]
    args += [k, v]
    if mode == "diff":
        in_specs += [pl.BlockSpec((4, X_HD), lambda b, h, i: (0, 0)), pl.BlockSpec((1, LANE), lambda b, h, i: (0, 0))]
        args += [lam, sub_g.reshape(1, LANE)]
        ow = LANE
    else:
        ow = 2 * LANE
    return pl.pallas_call(
        functools.partial(_flash_kernel, scale=d ** -0.5, tk=tk, n_lat=n_lat, mode=mode, lam_init=lam_init),
        grid=(B, n_kv, nq),
        in_specs=in_specs,
        out_specs=pl.BlockSpec((tq, ow), lambda b, h, i: (orow(b, h, i), h)),
        out_shape=jax.ShapeDtypeStruct((n_out, n_kv * ow), BF16),
        compiler_params=_cp(("parallel", "parallel", "arbitrary")),
        name="attn_" + mode + ("_ctx" if ctx_queries else ""),
    )(*args)


def _router_kernel(t_ref, r_ref, o_ref):
    t = t_ref[...]
    th, tm_, _ = _split3(t)
    rh, rm, _ = _split3(r_ref[...])
    d = lambda a, b: jnp.dot(a, b, preferred_element_type=F32)
    logits = (d(tm_, rh) + d(th, rm)) + d(th, rh)
    lane = lax.broadcasted_iota(jnp.int32, logits.shape, 1).astype(F32)
    logits = jnp.where(lane < N_EXPERTS, logits, -jnp.inf)
    v1 = jnp.max(logits, axis=-1, keepdims=True)
    i1 = jnp.min(jnp.where(logits == v1, lane, float(LANE)), axis=-1, keepdims=True)
    rest = jnp.where(lane == i1, -jnp.inf, logits)
    v2 = jnp.max(rest, axis=-1, keepdims=True)
    i2 = jnp.min(jnp.where(rest == v2, lane, float(LANE)), axis=-1, keepdims=True)
    e2 = jnp.exp(v2 - v1)
    den = 1.0 + e2
    out = jnp.where(lane == 0, i1, 0.0)
    out = jnp.where(lane == 1, i2, out)
    out = jnp.where(lane == 2, 1.0 / den, out)
    out = jnp.where(lane == 3, e2 / den, out)
    o_ref[...] = out


def router(t, r, tm):
    N, D = t.shape
    rp = jnp.zeros((D, LANE), F32).at[:, :N_EXPERTS].set(r)
    return pl.pallas_call(
        _router_kernel,
        grid=(N // tm,),
        in_specs=[pl.BlockSpec((tm, D), lambda m: (m, 0)), pl.BlockSpec((D, LANE), lambda m: (0, 0))],
        out_specs=pl.BlockSpec((tm, LANE), lambda m: (m, 0)),
        out_shape=jax.ShapeDtypeStruct((N, LANE), F32),
        compiler_params=_cp(("parallel",)),
        name="router",
    )(t, rp)


def _row_copy(src_hbm, dst_vmem, sem, src_row, dst_row):
    return pltpu.make_async_copy(src_hbm.at[pl.ds(src_row, 1), :], dst_vmem.at[pl.ds(dst_row, 1), :], sem)


def _gather_issue(src_hbm, idx_smem, dst_vmem, sem, n):
    def issue(r, c):
        _row_copy(src_hbm, dst_vmem, sem, idx_smem[0, 0, r], r).start()
        return c

    lax.fori_loop(0, n, issue, 0, unroll=8)


def _gather_drain(src_hbm, dst_vmem, sem, n):
    def drain(r, c):
        _row_copy(src_hbm, dst_vmem, sem, 0, r).wait()
        return c

    lax.fori_loop(0, n, drain, 0, unroll=8)


def _load_idx(idx_vmem, idx_smem, sem):
    cp = pltpu.make_async_copy(idx_vmem, idx_smem, sem)
    cp.start()
    cp.wait()


def _gather_kernel(idx_ref, src_ref, o_ref, idx_smem, buf, sem_i, sem_r, *, tg):
    _load_idx(idx_ref, idx_smem, sem_i)
    _gather_issue(src_ref, idx_smem, buf, sem_r, tg)
    _gather_drain(src_ref, buf, sem_r, tg)
    o_ref[...] = buf[...].astype(o_ref.dtype)


def gather_rows(src, idx, tg, out_dtype):
    P = idx.shape[0]
    D = src.shape[1]
    return pl.pallas_call(
        functools.partial(_gather_kernel, tg=tg),
        grid=(P // tg,),
        in_specs=[pl.BlockSpec((1, 1, tg), lambda i: (i, 0, 0)), pl.BlockSpec(memory_space=pl.ANY)],
        out_specs=pl.BlockSpec((tg, D), lambda i: (i, 0)),
        out_shape=jax.ShapeDtypeStruct((P, D), out_dtype),
        scratch_shapes=[pltpu.SMEM((1, 1, tg), jnp.int32), pltpu.VMEM((tg, D), src.dtype),
                        pltpu.SemaphoreType.DMA(()), pltpu.SemaphoreType.DMA(())],
        compiler_params=_cp(("arbitrary",)),
        name="moe_gather",
    )(idx.reshape(P // tg, 1, tg), src)


def _combine_kernel(p0_ref, p1_ref, ys_ref, x_ref, gate_ref, o_ref, s0, s1, b0, b1, sem_i, sem_r0, sem_r1, *, tg):
    _load_idx(p0_ref, s0, sem_i)
    _load_idx(p1_ref, s1, sem_i)
    _gather_issue(ys_ref, s0, b0, sem_r0, tg)
    _gather_issue(ys_ref, s1, b1, sem_r1, tg)
    _gather_drain(ys_ref, b0, sem_r0, tg)
    _gather_drain(ys_ref, b1, sem_r1, tg)
    o_ref[...] = x_ref[...] + gate_ref[...] * (b0[...] + b1[...])


def moe_combine(ys, pos0, pos1, x, mod4, geo, gate_idx, rows, tg):
    D = geo.D
    idx_spec = pl.BlockSpec((1, 1, tg), lambda i: (i, 0, 0))
    return pl.pallas_call(
        functools.partial(_combine_kernel, tg=tg),
        grid=(rows // tg,),
        in_specs=[idx_spec, idx_spec, pl.BlockSpec(memory_space=pl.ANY),
                  pl.BlockSpec((tg, D), lambda i: (i, 0)),
                  pl.BlockSpec((None, None, 1, D), lambda i: (geo.group(i * tg), gate_idx, 0, 0))],
        out_specs=pl.BlockSpec((tg, D), lambda i: (i, 0)),
        out_shape=jax.ShapeDtypeStruct((rows, D), F32),
        scratch_shapes=[pltpu.SMEM((1, 1, tg), jnp.int32), pltpu.SMEM((1, 1, tg), jnp.int32),
                        pltpu.VMEM((tg, D), F32), pltpu.VMEM((tg, D), F32),
                        pltpu.SemaphoreType.DMA(()), pltpu.SemaphoreType.DMA(()), pltpu.SemaphoreType.DMA(())],
        compiler_params=_cp(("arbitrary",)),
        name="moe_combine",
    )(pos0.reshape(rows // tg, 1, tg), pos1.reshape(rows // tg, 1, tg), ys, x, mod4)


def moe_ffn(h2, x, mod4, geo, gate_idx, r_w, w1, w3, w2, rows):
    D = geo.D
    E = w1.shape[0]
    F = w1.shape[2]
    tr = _pick(rows, (512, 256, 128, 64))
    route = router(h2, r_w, tr)
    e_flat = route[:, :TOP_K].astype(jnp.int32).reshape(-1)
    w_flat = route[:, TOP_K:2 * TOP_K].reshape(-1)
    n_assign = rows * TOP_K
    tm = _pick(rows, (512, 256, 128, 64))
    onehot = (e_flat[:, None] == jnp.arange(E, dtype=jnp.int32)[None, :]).astype(jnp.int32)
    csum = jnp.cumsum(onehot, axis=0)
    rank = jnp.sum(csum * onehot, axis=1) - 1
    counts = csum[-1]
    padded = (counts + tm - 1) // tm * tm
    pad_end = jnp.cumsum(padded)
    pad_start = pad_end - padded
    pos = pad_start[e_flat] + rank
    P = n_assign + E * tm
    n_tiles = P // tm
    tok_flat = jnp.repeat(jnp.arange(rows, dtype=jnp.int32), TOP_K)
    buf_tok = jnp.zeros((P,), jnp.int32).at[pos].set(tok_flat)
    buf_w = jnp.zeros((P,), F32).at[pos].set(w_flat)
    tile_e = jnp.minimum(jnp.searchsorted(pad_end, jnp.arange(n_tiles, dtype=jnp.int32) * tm, side="right"),
                         E - 1).astype(jnp.int32)
    n_used = (pad_end[-1] // tm).astype(jnp.int32).reshape(1)

    tg = _pick(rows, (256, 128, 64))
    xs = gather_rows(h2, buf_tok, tg, BF16)
    tn1 = _pick(F, (512, 256, 128))
    hb = grouped_mm(xs, [w1, w3], tile_e, n_used, tm=tm, tn=tn1, tk=D, epi="swiglu", out_dtype=BF16, name="moe_up")
    tn2 = _pick(D, (512, 256, 128))
    ys = grouped_mm(hb, [w2], tile_e, n_used, tm=tm, tn=tn2, tk=F, epi="rowscale", out_dtype=F32,
                    row_w=buf_w.reshape(P, 1), name="moe_down")
    pos2 = pos.reshape(rows, TOP_K)
    return moe_combine(ys, pos2[:, 0], pos2[:, 1], x, mod4, geo, gate_idx, rows, tg)


def _pack_w_in(w):
    D = w.shape[0]
    cols, at = [], 0
    for piece, off, width in sorted(_Z.values(), key=lambda v: v[1]):
        assert off == at
        lo, hi = _REF_OFF[piece], _REF_OFF[piece + 1]
        cols.append(w[:, lo:hi].astype(BF16))
        if width > hi - lo:
            cols.append(jnp.zeros((D, width - (hi - lo)), BF16))
        at += width
    assert at == Z_W
    return jnp.concatenate(cols, axis=1)


def kernel(x, c, ctx, c_ctx, ada_w, ada_b, norm1_g, norm2_g, w_in, mlstm_conv_w, mlstm_conv_b, mlstm_gate_b, mlstm_norm_g, conv_dw_w, conv_dw_b, conv_ln_g, conv_ln_b, gqa_q_norm_g, gqa_k_norm_g, diff_q_norm_g, diff_k_norm_g, diff_lambda, diff_subln_g, merge_gate_w, merge_gate_b, branch_w, out_w, ffn_w1, ffn_w3, ffn_w2, moe_router, moe_w1, moe_w3, moe_w2):
    B, S, D = x.shape
    L = ctx.shape[1]
    depth = ada_w.shape[0]
    geo = Geo(B, S, L, D)
    N, BS = geo.N, geo.BS
    assert S % GRID_W == 0 and S % M_CHUNK == 0 and L % M_CHUNK == 0 and BS % L == 0
    tb = _pick(math.gcd(S, L), (256, 128, 64))
    tm = _pick(math.gcd(S, B * L), (1024, 512, 256, 128))

    xt = jnp.concatenate([x.reshape(BS, D), ctx.reshape(B * L, D)], axis=0)
    cc = jnp.zeros((8, D), F32).at[:B].set(c).at[B].set(c_ctx)

    for l in range(depth):
        need_ctx = l < depth - 1
        rows = N if need_ctx else BS
        lam_init = 0.8 - 0.6 * math.exp(-0.3 * l)
        mod = adaln(cc, ada_w[l], ada_b[l])
        mod4 = mod[:B + 1].reshape(B + 1, 6, 1, D)

        h = norm_mod(xt, norm1_g[l], mod4, geo, 0, 1, N, BF16)
        z = dense_mm(h, [_pack_w_in(w_in[l])], rows=N, tm=tm, tn=_pick(Z_W, (640, 128)), tk=D, order="mn",
                     epi="none", out_dtype=F32, name="w_in")

        mq, mk = mlstm_qk(z, mlstm_conv_w[l], mlstm_conv_b[l], geo, tb)
        bc, cg = mlstm_gates(z, mlstm_gate_b[l], geo, tb)
        lanes = np.array([_gate_lane(d, hh) for d in range(2) for hh in range(M_HEADS)])
        cr = cg[:, lanes].reshape(N // M_CHUNK, M_CHUNK, 2 * M_HEADS).transpose(0, 2, 1)
        hf, hbw = mlstm_scan(mq, mk, z, bc, cg, cr, geo, tb)
        a_br = mlstm_out(hf, hbw, z, mlstm_norm_g[l], rows, tb)

        b_br = conformer(z, conv_dw_w[l], conv_dw_b[l], conv_ln_g[l], conv_ln_b[l], geo, tb)

        gq, gk, gv, xq, xk, xv = attn_prep(z, gqa_q_norm_g[l], gqa_k_norm_g[l], diff_q_norm_g[l], diff_k_norm_g[l],
                                           geo, tb)
        dkw = dict(lam=diff_lambda[l], sub_g=diff_subln_g[l], lam_init=lam_init)
        c_br = attention(gq, gk, gv, geo, n_kv=G_KV, d=G_HD, mode="gqa", ctx_queries=False)
        d_br = attention(xq, xk, xv, geo, n_kv=X_HEADS, d=X_HD, mode="diff", ctx_queries=False, **dkw)
        if need_ctx:
            c_ctx_o = attention(gq, gk, gv, geo, n_kv=G_KV, d=G_HD, mode="gqa", ctx_queries=True)
            d_ctx_o = attention(xq, xk, xv, geo, n_kv=X_HEADS, d=X_HD, mode="diff", ctx_queries=True, **dkw)
            c_br = jnp.concatenate([c_br, c_ctx_o], axis=0)
            d_br = jnp.concatenate([d_br, d_ctx_o], axis=0)

        tn = _pick(D, (512, 256, 128))
        y = merge(h, [a_br, b_br, c_br, d_br], merge_gate_w[l].astype(BF16), merge_gate_b[l].reshape(4, 1, D),
                  branch_w[l].astype(BF16), rows=rows, tm=tm, tn=tn)
        xt = dense_mm(y, [out_w[l].astype(BF16)], rows=rows, tm=tm, tn=tn, tk=D, order="mn", epi="resid",
                      out_dtype=F32, geo=geo, res=xt, mod4=mod4, gate_idx=2, name="out_proj")

        i = l // 2
        if l % 2 == 0:
            h2 = norm_mod(xt, norm2_g[l], mod4, geo, 3, 4, rows, BF16)
            F = ffn_w1.shape[2]
            tnf = _pick(F, (512, 256, 128))
            hb = dense_mm(h2, [ffn_w1[i].astype(BF16), ffn_w3[i].astype(BF16)], rows=rows, tm=tm, tn=tnf, tk=D,
                          order="mn", epi="swiglu", out_dtype=BF16, name="ffn_up")
            xt = dense_mm(hb, [ffn_w2[i].astype(BF16)], rows=rows, tm=tm, tn=_pick(D, (1024, 512, 256, 128)),
                          tk=_pick(F, (512, 256, 128)),
                          order="mn", epi="resid", out_dtype=F32, geo=geo, res=xt, mod4=mod4, gate_idx=5,
                          name="ffn_down")
        else:
            h2 = norm_mod(xt, norm2_g[l], mod4, geo, 3, 4, rows, F32)
            xt = moe_ffn(h2, xt, mod4, geo, 5, moe_router[i], moe_w1[i].astype(BF16), moe_w3[i].astype(BF16),
                         moe_w2[i].astype(BF16), rows)
    return xt[:BS].reshape(B, S, D)
```

```python
import functools
import math

import numpy as np
import jax
import jax.numpy as jnp
from jax import lax
from jax.experimental import pallas as pl
from jax.experimental.pallas import tpu as pltpu

F32 = jnp.float32
BF16 = jnp.bfloat16

GRID_W = 64
M_HEADS, M_HD, M_CHUNK, M_CONV = 4, 128, 64, 3
M_W = M_HEADS * M_HD
C_W, C_K = 512, 31
G_HEADS, G_KV, G_HD = 4, 2, 128
X_HEADS, X_HD = 4, 64
X_VD = 2 * X_HD
ROPE_THETA = 10000.0
N_EXPERTS, TOP_K = 8, 2
EPS = 1e-6
LANE = 128

_REF_SPLITS = (2 * M_W, M_W, M_W, 4 * M_HEADS, 2 * C_W, G_HEADS * G_HD, G_KV * G_HD, G_KV * G_HD,
               X_HEADS * 2 * X_HD, X_HEADS * 2 * X_HD, X_HEADS * X_VD)
_REF_OFF = np.concatenate([[0], np.cumsum(_REF_SPLITS)]).tolist()
_Z = {
    "m_qk": (0, 0, 1024), "c_glu": (4, 1024, 1024), "m_v": (1, 2048, 512), "m_o": (2, 2560, 512),
    "g_q": (5, 3072, 512), "x_q": (8, 3584, 512), "x_k": (9, 4096, 512), "x_v": (10, 4608, 512),
    "g_k": (6, 5120, 256), "g_v": (7, 5376, 256), "m_g": (3, 5632, 128),
}
Z_W = 5760

_VMEM_LIMIT = 56 * 1024 * 1024


def _cp(sem, vmem=_VMEM_LIMIT):
    return pltpu.CompilerParams(dimension_semantics=sem, vmem_limit_bytes=vmem)


def _pick(n, cands):
    for c in cands:
        if n % c == 0:
            return c
    raise ValueError(f"no tile for {n} in {cands}")


def _sigmoid(x):
    return 1.0 / (1.0 + jnp.exp(-x))


def _silu(x):
    return x * _sigmoid(x)


def _adaln_kernel(c_ref, w_ref, b_ref, o_ref):
    s = _silu(c_ref[...]).astype(BF16)
    o_ref[...] = jnp.dot(s, w_ref[...].astype(BF16), preferred_element_type=F32) + b_ref[...]


def adaln(cc, w, b):
    R, D = cc.shape
    N = w.shape[1]
    tn = _pick(N, (512, 256, 128))
    return pl.pallas_call(
        _adaln_kernel,
        grid=(N // tn,),
        in_specs=[pl.BlockSpec((R, D), lambda n: (0, 0)),
                  pl.BlockSpec((D, tn), lambda n: (0, n)),
                  pl.BlockSpec((1, tn), lambda n: (0, n))],
        out_specs=pl.BlockSpec((R, tn), lambda n: (0, n)),
        out_shape=jax.ShapeDtypeStruct((R, N), F32),
        compiler_params=_cp(("parallel",)),
        name="adaln",
    )(cc, w, b.reshape(1, N))


class Geo:
    def __init__(self, B, S, L, D):
        self.B, self.S, self.L, self.D = B, S, L, D
        self.BS = B * S
        self.N = B * S + B * L

    def group(self, row0):
        return jnp.minimum(row0 // self.S, self.B)


def _norm_mod_kernel(x_ref, g_ref, sh_ref, sc_ref, o_ref):
    x = x_ref[...]
    y = x * lax.rsqrt(jnp.mean(x * x, axis=-1, keepdims=True) + EPS)
    y = y * g_ref[...]
    o_ref[...] = (y * (1.0 + sc_ref[...]) + sh_ref[...]).astype(o_ref.dtype)


def norm_mod(x, g, mod4, geo, i_shift, i_scale, rows, out_dtype):
    D = geo.D
    tm = _pick(math.gcd(geo.S, geo.B * geo.L), (256, 128, 64))
    return pl.pallas_call(
        _norm_mod_kernel,
        grid=(rows // tm,),
        in_specs=[pl.BlockSpec((tm, D), lambda m: (m, 0)),
                  pl.BlockSpec((1, D), lambda m: (0, 0)),
                  pl.BlockSpec((None, None, 1, D), lambda m: (geo.group(m * tm), i_shift, 0, 0)),
                  pl.BlockSpec((None, None, 1, D), lambda m: (geo.group(m * tm), i_scale, 0, 0))],
        out_specs=pl.BlockSpec((tm, D), lambda m: (m, 0)),
        out_shape=jax.ShapeDtypeStruct((rows, D), out_dtype),
        compiler_params=_cp(("parallel",)),
        name="norm_mod",
    )(x, g.reshape(1, D), mod4, mod4)


def _bf16(x):
    return x if x.dtype == BF16 else x.astype(BF16)


def _mm_body(a_ref, w_refs, o_ref, acc_refs, nk, finalize):
    k = pl.program_id(2)
    a = _bf16(a_ref[...])
    parts = [jnp.dot(a, _bf16(w[...]), preferred_element_type=F32) for w in w_refs]
    if nk == 1:
        finalize(parts)
        return

    @pl.when(k == 0)
    def _():
        for acc, p in zip(acc_refs, parts):
            acc[...] = p

    @pl.when(k > 0)
    def _():
        for acc, p in zip(acc_refs, parts):
            acc[...] += p

    @pl.when(k == nk - 1)
    def _():
        finalize([acc[...] for acc in acc_refs])


def _dense_mm_kernel(*refs, n_w, nk, epi):
    refs = list(refs)
    a_ref = refs.pop(0)
    w_refs = [refs.pop(0) for _ in range(n_w)]
    extra = []
    if epi == "resid":
        extra = [refs.pop(0), refs.pop(0)]
    o_ref = refs.pop(0)

    def finalize(parts):
        if epi == "none":
            o_ref[...] = parts[0].astype(o_ref.dtype)
        elif epi == "swiglu":
            o_ref[...] = (_silu(parts[0]) * parts[1]).astype(o_ref.dtype)
        else:
            o_ref[...] = (extra[0][...] + extra[1][...] * parts[0]).astype(o_ref.dtype)

    _mm_body(a_ref, w_refs, o_ref, refs, nk, finalize)


def dense_mm(a, ws, *, rows, tm, tn, tk, order, epi, out_dtype, geo=None, res=None, mod4=None, gate_idx=None, name="mm"):
    K = a.shape[1]
    N = ws[0].shape[1]
    nm, nn, nk = rows // tm, N // tn, K // tk
    assert rows % tm == 0 and N % tn == 0 and K % tk == 0
    if order == "mn":
        grid = (nm, nn, nk)
        mi = lambda g0, g1: g0
        ni = lambda g0, g1: g1
    else:
        grid = (nn, nm, nk)
        mi = lambda g0, g1: g1
        ni = lambda g0, g1: g0
    in_specs = [pl.BlockSpec((tm, tk), lambda g0, g1, k: (mi(g0, g1), k))]
    in_specs += [pl.BlockSpec((tk, tn), lambda g0, g1, k: (k, ni(g0, g1))) for _ in ws]
    args = [a, *ws]
    if epi == "resid":
        in_specs.append(pl.BlockSpec((tm, tn), lambda g0, g1, k: (mi(g0, g1), ni(g0, g1))))
        in_specs.append(pl.BlockSpec((None, None, 1, tn),
                                     lambda g0, g1, k: (geo.group(mi(g0, g1) * tm), gate_idx, 0, ni(g0, g1))))
        args += [res, mod4]
    scratch = [pltpu.VMEM((tm, tn), F32) for _ in ws] if nk > 1 else []
    return pl.pallas_call(
        functools.partial(_dense_mm_kernel, n_w=len(ws), nk=nk, epi=epi),
        grid=grid,
        in_specs=in_specs,
        out_specs=pl.BlockSpec((tm, tn), lambda g0, g1, k: (mi(g0, g1), ni(g0, g1))),
        out_shape=jax.ShapeDtypeStruct((rows, N), out_dtype),
        scratch_shapes=scratch,
        compiler_params=_cp(("parallel", "parallel", "arbitrary")),
        name=name,
    )(*args)


def _grouped_mm_kernel(te_ref, nu_ref, *refs, n_w, nk, epi):
    refs = list(refs)
    a_ref = refs.pop(0)
    w_refs = [refs.pop(0) for _ in range(n_w)]
    o_ref = refs.pop(0)
    m = pl.program_id(1)
    k = pl.program_id(2)

    def finalize(parts):
        if epi == "swiglu":
            o_ref[...] = (_silu(parts[0]) * parts[1]).astype(o_ref.dtype)
        else:
            o_ref[...] = parts[0].astype(o_ref.dtype)

    @pl.when(m < nu_ref[0])
    def _():
        _mm_body(a_ref, w_refs, o_ref, refs, nk, finalize)

    @pl.when(jnp.logical_and(m >= nu_ref[0], k == nk - 1))
    def _():
        o_ref[...] = jnp.zeros_like(o_ref)


def grouped_mm(a, ws, tile_e, n_used, *, tm, tn, tk, epi, out_dtype, name="gmm"):
    P, K = a.shape
    N = ws[0].shape[2]
    nm, nn, nk = P // tm, N // tn, K // tk
    assert P % tm == 0 and N % tn == 0 and K % tk == 0

    def mc(m, nu):
        return jnp.minimum(m, nu[0] - 1)

    in_specs = [pl.BlockSpec((tm, tk), lambda n, m, k, te, nu: (mc(m, nu), k))]
    in_specs += [pl.BlockSpec((None, tk, tn), lambda n, m, k, te, nu: (te[mc(m, nu)], k, n)) for _ in ws]
    args = [a, *ws]
    scratch = [pltpu.VMEM((tm, tn), F32) for _ in ws] if nk > 1 else []
    return pl.pallas_call(
        functools.partial(_grouped_mm_kernel, n_w=len(ws), nk=nk, epi=epi),
        grid_spec=pltpu.PrefetchScalarGridSpec(
            num_scalar_prefetch=2,
            grid=(nn, nm, nk),
            in_specs=in_specs,
            out_specs=pl.BlockSpec((tm, tn), lambda n, m, k, te, nu: (m, n)),
            scratch_shapes=scratch),
        out_shape=jax.ShapeDtypeStruct((P, N), out_dtype),
        compiler_params=_cp(("parallel", "arbitrary", "arbitrary")),
        name=name,
    )(tile_e, n_used, *args)


def _merge_kernel(h_ref, wg_ref, bg_ref, b0_ref, b1_ref, b2_ref, b3_ref, wb_ref, o_ref):
    h = h_ref[...]
    acc = None
    for j, br in enumerate((b0_ref, b1_ref, b2_ref, b3_ref)):
        gate = _sigmoid(jnp.dot(h, wg_ref[j], preferred_element_type=F32) + bg_ref[j])
        term = gate * jnp.dot(br[...], wb_ref[j], preferred_element_type=F32)
        acc = term if acc is None else acc + term
    o_ref[...] = acc.astype(o_ref.dtype)


def merge(h, branches, wg, bg, wb, *, rows, tm, tn):
    D = h.shape[1]
    BW = branches[0].shape[1]
    nb = len(branches)
    br_spec = pl.BlockSpec((tm, BW), lambda m, n: (m, 0))
    return pl.pallas_call(
        _merge_kernel,
        grid=(rows // tm, D // tn),
        in_specs=[pl.BlockSpec((tm, D), lambda m, n: (m, 0)),
                  pl.BlockSpec((nb, D, tn), lambda m, n: (0, 0, n)),
                  pl.BlockSpec((nb, 1, tn), lambda m, n: (0, 0, n)),
                  br_spec, br_spec, br_spec, br_spec,
                  pl.BlockSpec((nb, BW, tn), lambda m, n: (0, 0, n))],
        out_specs=pl.BlockSpec((tm, tn), lambda m, n: (m, n)),
        out_shape=jax.ShapeDtypeStruct((rows, D), BF16),
        compiler_params=_cp(("parallel", "parallel")),
        name="merge",
    )(h, wg, bg, *branches, wb)


def _seq_flags(geo, tb):
    row0 = pl.program_id(0) * tb
    lat = row0 < geo.BS
    rel = row0 - geo.BS
    start = jnp.where(lat, row0 % geo.S == 0, rel % geo.L == 0)
    end = jnp.where(lat, (row0 + tb) % geo.S == 0, (rel + tb) % geo.L == 0)
    return start, end


def _halo_specs(geo, tb, hb, width, col_block):
    r = tb // hb
    last = geo.N // hb - 1
    return [pl.BlockSpec((tb, width), lambda i: (i, col_block)),
            pl.BlockSpec((hb, width), lambda i: (jnp.maximum(i * r - 1, 0), col_block)),
            pl.BlockSpec((hb, width), lambda i: (jnp.minimum((i + 1) * r, last), col_block))]


def _conv_rows(u_ref, w_ref, base, r0, rows, K):
    acc = None
    for j in range(K):
        t = w_ref[j:j + 1, :] * u_ref[pl.ds(base + r0 + j, rows), :]
        acc = t if acc is None else acc + t
    return acc


_C_HB = 16


def _conformer_kernel(main_ref, prev_ref, next_ref, w_ref, b_ref, lg_ref, lb_ref, o_ref, u_ref, *, geo, tb):
    start, end = _seq_flags(geo, tb)

    def glu(blk):
        return blk[:, :C_W] * _sigmoid(blk[:, C_W:])

    u_ref[pl.ds(_C_HB, tb), :] = glu(main_ref[...])
    u_ref[pl.ds(0, _C_HB), :] = jnp.where(start, 0.0, glu(prev_ref[...]))
    u_ref[pl.ds(_C_HB + tb, _C_HB), :] = jnp.where(end, 0.0, glu(next_ref[...]))
    rc = 32
    for r0 in range(0, tb, rc):
        u = _conv_rows(u_ref, w_ref, _C_HB - C_K // 2, r0, rc, C_K) + b_ref[...]
        mu = jnp.mean(u, axis=-1, keepdims=True)
        xc = u - mu
        y = xc * lax.rsqrt(jnp.mean(xc * xc, axis=-1, keepdims=True) + EPS)
        y = y * lg_ref[...] + lb_ref[...]
        o_ref[pl.ds(r0, rc), :] = _silu(y).astype(o_ref.dtype)


def conformer(z, w, b, lg, lb, geo, tb):
    cb = _Z["c_glu"][1] // (2 * C_W)
    return pl.pallas_call(
        functools.partial(_conformer_kernel, geo=geo, tb=tb),
        grid=(geo.N // tb,),
        in_specs=_halo_specs(geo, tb, _C_HB, 2 * C_W, cb) + [
            pl.BlockSpec((C_K, C_W), lambda i: (0, 0)),
            pl.BlockSpec((1, C_W), lambda i: (0, 0)),
            pl.BlockSpec((1, C_W), lambda i: (0, 0)),
            pl.BlockSpec((1, C_W), lambda i: (0, 0))],
        out_specs=pl.BlockSpec((tb, C_W), lambda i: (i, 0)),
        out_shape=jax.ShapeDtypeStruct((geo.N, C_W), BF16),
        scratch_shapes=[pltpu.VMEM((tb + 2 * _C_HB, C_W), F32)],
        compiler_params=_cp(("parallel",)),
        name="conformer",
    )(z, z, z, w, b.reshape(1, C_W), lg.reshape(1, C_W), lb.reshape(1, C_W))


_M_HB = 8


def _mlstm_qk_kernel(main_ref, prev_ref, next_ref, w_ref, b_ref, q_ref, k_ref, u_ref, *, geo, tb):
    start, end = _seq_flags(geo, tb)
    u_ref[pl.ds(_M_HB, tb), :] = main_ref[...]
    u_ref[pl.ds(0, _M_HB), :] = jnp.where(start, 0.0, prev_ref[...])
    u_ref[pl.ds(_M_HB + tb, _M_HB), :] = jnp.where(end, 0.0, next_ref[...])
    rc = 32
    for r0 in range(0, tb, rc):
        y = _silu(_conv_rows(u_ref, w_ref, _M_HB - M_CONV // 2, r0, rc, M_CONV) + b_ref[...])
        q_ref[pl.ds(r0, rc), :] = y[:, :M_W].astype(q_ref.dtype)
        k_ref[pl.ds(r0, rc), :] = (y[:, M_W:] * (M_HD ** -0.5)).astype(k_ref.dtype)


def mlstm_qk(z, w, b, geo, tb):
    cb = _Z["m_qk"][1] // (2 * M_W)
    return pl.pallas_call(
        functools.partial(_mlstm_qk_kernel, geo=geo, tb=tb),
        grid=(geo.N // tb,),
        in_specs=_halo_specs(geo, tb, _M_HB, 2 * M_W, cb) + [
            pl.BlockSpec((M_CONV, 2 * M_W), lambda i: (0, 0)),
            pl.BlockSpec((1, 2 * M_W), lambda i: (0, 0))],
        out_specs=[pl.BlockSpec((tb, M_W), lambda i: (i, 0)), pl.BlockSpec((tb, M_W), lambda i: (i, 0))],
        out_shape=[jax.ShapeDtypeStruct((geo.N, M_W), BF16), jax.ShapeDtypeStruct((geo.N, M_W), BF16)],
        scratch_shapes=[pltpu.VMEM((tb + 2 * _M_HB, 2 * M_W), F32)],
        compiler_params=_cp(("parallel",)),
        name="mlstm_qk",
    )(z, z, z, w, b.reshape(1, 2 * M_W))


def _split3(x):
    hi = x.astype(BF16)
    r1 = x - hi.astype(F32)
    mid = r1.astype(BF16)
    lo = (r1 - mid.astype(F32)).astype(BF16)
    return hi, mid, lo


def _log_sigmoid(x):
    return jnp.minimum(x, 0.0) - jnp.log(1.0 + jnp.exp(-jnp.abs(x)))


def _mlstm_gate_kernel(g_ref, b_ref, bc_ref, cc_ref, *, tb):
    raw = g_ref[...] + b_ref[...]
    lf = _log_sigmoid(raw)
    t = lax.broadcasted_iota(jnp.int32, (tb, tb), 0)
    s = lax.broadcasted_iota(jnp.int32, (tb, tb), 1)
    same = (t // M_CHUNK) == (s // M_CHUNK)
    lower = jnp.where(jnp.logical_and(same, s <= t), 1.0, 0.0).astype(BF16)
    upper = jnp.where(jnp.logical_and(same, s >= t), 1.0, 0.0).astype(BF16)
    hi, mid, lo = _split3(lf)

    def csum(tri):
        d = lambda p: jnp.dot(tri, p, preferred_element_type=F32)
        return (d(lo) + d(mid)) + d(hi)

    lane = lax.broadcasted_iota(jnp.int32, (tb, LANE), 1)
    b_all = jnp.where(lane < 2 * M_HEADS, csum(lower), csum(upper))
    i_sh = pltpu.roll(raw, M_HEADS, axis=1)
    bc_ref[...] = b_all
    cc_ref[...] = i_sh - b_all


def mlstm_gates(z, gate_b, geo, tb):
    cb = _Z["m_g"][1] // LANE
    gb = jnp.zeros((1, LANE), F32).at[0, :4 * M_HEADS].set(gate_b)
    spec = pl.BlockSpec((tb, LANE), lambda i: (i, 0))
    return pl.pallas_call(
        functools.partial(_mlstm_gate_kernel, tb=tb),
        grid=(geo.N // tb,),
        in_specs=[pl.BlockSpec((tb, LANE), lambda i: (i, cb)), pl.BlockSpec((1, LANE), lambda i: (0, 0))],
        out_specs=[spec, spec],
        out_shape=[jax.ShapeDtypeStruct((geo.N, LANE), F32)] * 2,
        compiler_params=_cp(("parallel",)),
        name="mlstm_gates",
    )(z, gb)


def _gate_lane(d, h):
    return 4 + 8 * d + h


def _mlstm_scan_kernel(*refs, tb):
    (qf, kf, vf, bcf, ccf, crf, qb, kb, vb, bcb, ccb, crb, hf_ref, hb_ref, ct_ref, m_ref) = refs
    i = pl.program_id(1)

    @pl.when(i == 0)
    def _():
        ct_ref[...] = jnp.zeros_like(ct_ref)
        m_ref[...] = jnp.zeros_like(m_ref)

    nc = tb // M_CHUNK
    t_i = lax.broadcasted_iota(jnp.int32, (M_CHUNK, M_CHUNK), 0)
    s_i = lax.broadcasted_iota(jnp.int32, (M_CHUNK, M_CHUNK), 1)
    e0 = jnp.where(lax.broadcasted_iota(jnp.int32, (M_CHUNK, M_HD), 1) == 0, 1.0, 0.0)
    dirs = ((0, qf, kf, vf, bcf, ccf, crf, hf_ref, s_i <= t_i), (1, qb, kb, vb, bcb, ccb, crb, hb_ref, s_i >= t_i))
    for d, q_ref, k_ref, v_ref, bc_ref, cc_ref, cr_ref, h_ref, mask in dirs:
        chunks = range(nc) if d == 0 else range(nc - 1, -1, -1)
        last = M_CHUNK - 1 if d == 0 else 0
        for h in range(M_HEADS):
            ln = _gate_lane(d, h)
            sid = d * M_HEADS + h
            m_prev = m_ref[sid, 0:1, 0:1]
            ct = ct_ref[sid]
            for c in chunks:
                rows = pl.ds(c * M_CHUNK, M_CHUNK)
                lanes = slice(h * M_HD, (h + 1) * M_HD)
                qc = q_ref[rows, lanes]
                kc = k_ref[rows, lanes]
                vc = v_ref[rows, lanes]
                col_r = cr_ref[c, sid:sid + 1, :]
                col_c = cc_ref[rows, ln:ln + 1]
                b_c = bc_ref[rows, ln:ln + 1]
                cm = jnp.max(jnp.where(mask, col_r, -jnp.inf), axis=-1, keepdims=True)
                mt = jnp.maximum(m_prev, cm)
                s = lax.dot_general(qc, kc, (((1,), (1,)), ((), ())), preferred_element_type=F32)
                w = jnp.where(mask, jnp.exp(col_r - mt), 0.0) * s
                v_aug = jnp.concatenate([vc, e0], axis=1)
                tot = jnp.exp(m_prev - mt) * jnp.dot(qc, ct.astype(BF16), preferred_element_type=F32)
                tot = tot + jnp.dot(w.astype(BF16), v_aug.astype(BF16), preferred_element_type=F32)
                den = jnp.maximum(jnp.abs(tot[:, M_HD:M_HD + 1]), jnp.exp(-b_c - mt))
                h_ref[rows, lanes] = tot[:, :M_HD] / den
                m_end = mt[last:last + 1, :]
                wg = jnp.exp(col_c - m_end)
                upd = lax.dot_general(kc, (wg * v_aug).astype(BF16), (((0,), (0,)), ((), ())),
                                      preferred_element_type=F32)
                ct = jnp.exp(m_prev - m_end) * ct + upd
                m_prev = b_c[last:last + 1, :] + m_end
            ct_ref[sid] = ct
            m_ref[sid] = jnp.broadcast_to(m_prev, m_ref.shape[1:])


def mlstm_scan(q, k, z, bc, cc, cr, geo, tb):
    B, S, L = geo.B, geo.S, geo.L
    nctx, nlat = L // tb, S // tb
    nblk = nctx + nlat
    vcb = _Z["m_v"][1] // M_W

    def blk(b, i, d):
        ctx_i = i if d == 0 else nctx - 1 - i
        lat_i = i - nctx if d == 0 else nlat - 1 - (i - nctx)
        return jnp.where(i < nctx, (B * S) // tb + b * nctx + ctx_i, b * nlat + lat_i)

    def specs(d):
        return [pl.BlockSpec((tb, M_W), lambda b, i: (blk(b, i, d), 0)),
                pl.BlockSpec((tb, M_W), lambda b, i: (blk(b, i, d), 0)),
                pl.BlockSpec((tb, M_W), lambda b, i: (blk(b, i, d), vcb)),
                pl.BlockSpec((tb, LANE), lambda b, i: (blk(b, i, d), 0)),
                pl.BlockSpec((tb, LANE), lambda b, i: (blk(b, i, d), 0)),
                pl.BlockSpec((tb // M_CHUNK, 2 * M_HEADS, M_CHUNK), lambda b, i: (blk(b, i, d), 0, 0))]

    out_f = pl.BlockSpec((tb, M_W), lambda b, i: (blk(b, i, 0), 0))
    out_b = pl.BlockSpec((tb, M_W), lambda b, i: (blk(b, i, 1), 0))
    return pl.pallas_call(
        functools.partial(_mlstm_scan_kernel, tb=tb),
        grid=(B, nblk),
        in_specs=specs(0) + specs(1),
        out_specs=[out_f, out_b],
        out_shape=[jax.ShapeDtypeStruct((geo.N, M_W), F32)] * 2,
        scratch_shapes=[pltpu.VMEM((2 * M_HEADS, M_HD, 2 * M_HD), F32), pltpu.VMEM((2 * M_HEADS, 8, LANE), F32)],
        compiler_params=_cp(("parallel", "arbitrary")),
        name="mlstm_scan",
    )(q, k, z, bc, cc, cr, q, k, z, bc, cc, cr)


def _mlstm_out_kernel(hf_ref, hb_ref, o_ref, g_ref, a_ref):
    hsum = hf_ref[...] + hb_ref[...]
    gate = _sigmoid(o_ref[...])
    for h in range(M_HEADS):
        lanes = slice(h * M_HD, (h + 1) * M_HD)
        x = hsum[:, lanes]
        y = x * lax.rsqrt(jnp.mean(x * x, axis=-1, keepdims=True) + EPS) * g_ref[:, lanes]
        a_ref[:, lanes] = (y * gate[:, lanes]).astype(a_ref.dtype)


def mlstm_out(hf, hb, z, g, rows, tb):
    ocb = _Z["m_o"][1] // M_W
    spec = pl.BlockSpec((tb, M_W), lambda i: (i, 0))
    return pl.pallas_call(
        _mlstm_out_kernel,
        grid=(rows // tb,),
        in_specs=[spec, spec, pl.BlockSpec((tb, M_W), lambda i: (i, ocb)), pl.BlockSpec((1, M_W), lambda i: (0, 0))],
        out_specs=spec,
        out_shape=jax.ShapeDtypeStruct((rows, M_W), BF16),
        compiler_params=_cp(("parallel",)),
        name="mlstm_out",
    )(hf, hb, z, g.reshape(1, M_W))


def _rope_tables(S, d):
    r = d // 4
    t = np.arange(S)
    freqs = ROPE_THETA ** (-np.arange(r, dtype=np.float32) / r)
    rows = (t // GRID_W).astype(np.float32)[:, None] * freqs
    cols = (t % GRID_W).astype(np.float32)[:, None] * freqs
    rows, cols = jnp.asarray(rows, F32), jnp.asarray(cols, F32)
    cos = jnp.concatenate([jnp.cos(rows), jnp.cos(rows), jnp.cos(cols), jnp.cos(cols)], axis=1)
    sin = jnp.concatenate([-jnp.sin(rows), jnp.sin(rows), -jnp.sin(cols), jnp.sin(cols)], axis=1)
    rep = LANE // d
    return jnp.tile(cos, (1, rep)), jnp.tile(sin, (1, rep))


def _norm_rope(x, g, cos, sin, is_lat, d):
    r = d // 4
    if d == LANE:
        ms = jnp.mean(x * x, axis=-1, keepdims=True)
    else:
        parts = [jnp.broadcast_to(jnp.mean(x[:, o:o + d] * x[:, o:o + d], axis=-1, keepdims=True), (x.shape[0], d))
                 for o in range(0, LANE, d)]
        ms = jnp.concatenate(parts, axis=1)
    y = x * lax.rsqrt(ms + EPS) * g
    lane = lax.broadcasted_iota(jnp.int32, y.shape, 1)
    first = (lane // r) % 2 == 0
    partner = jnp.where(first, pltpu.roll(y, LANE - r, axis=1), pltpu.roll(y, r, axis=1))
    return jnp.where(is_lat, y * cos + partner * sin, y)


def _attn_prep_kernel(gq_ref, gk_ref, gv_ref, xq_ref, xk_ref, xv_ref, gqg_ref, gkg_ref, xqg_ref, xkg_ref,
                      cg_ref, sg_ref, cx_ref, sx_ref, oq_ref, ok_ref, ov_ref, oxq_ref, oxk_ref, oxv_ref, *, geo, tb):
    is_lat = pl.program_id(0) * tb < geo.BS
    cg, sg, cx, sx = cg_ref[...], sg_ref[...], cx_ref[...], sx_ref[...]
    for h in range(G_HEADS):
        ls = slice(h * LANE, (h + 1) * LANE)
        oq_ref[:, ls] = _norm_rope(gq_ref[:, ls], gqg_ref[...], cg, sg, is_lat, G_HD).astype(BF16)
    for h in range(G_KV):
        ls = slice(h * LANE, (h + 1) * LANE)
        ok_ref[:, ls] = _norm_rope(gk_ref[:, ls], gkg_ref[...], cg, sg, is_lat, G_HD).astype(BF16)
    ov_ref[...] = gv_ref[...].astype(BF16)
    lane = lax.broadcasted_iota(jnp.int32, (tb, LANE), 1)
    for h in range(X_HEADS):
        ls = slice(h * LANE, (h + 1) * LANE)
        q = _norm_rope(xq_ref[:, ls], xqg_ref[...], cx, sx, is_lat, X_HD)
        oxq_ref[:, 2 * h * LANE:(2 * h + 1) * LANE] = jnp.where(lane < X_HD, q, 0.0).astype(BF16)
        oxq_ref[:, (2 * h + 1) * LANE:(2 * h + 2) * LANE] = jnp.where(lane >= X_HD, q, 0.0).astype(BF16)
        oxk_ref[:, ls] = _norm_rope(xk_ref[:, ls], xkg_ref[...], cx, sx, is_lat, X_HD).astype(BF16)
    oxv_ref[...] = xv_ref[...].astype(BF16)


def attn_prep(z, gq_g, gk_g, xq_g, xk_g, geo, tb):
    N = geo.N
    cg, sg = _rope_tables(geo.S, G_HD)
    cx, sx = _rope_tables(geo.S, X_HD)
    nlat_t = geo.S // tb

    def zs(name, width):
        cb = _Z[name][1] // width
        return pl.BlockSpec((tb, width), lambda i: (i, cb))

    tab = pl.BlockSpec((tb, LANE), lambda i: (i % nlat_t, 0))
    vec = pl.BlockSpec((1, LANE), lambda i: (0, 0))
    row = lambda w: pl.BlockSpec((tb, w), lambda i: (i, 0))
    rep = LANE // X_HD
    return pl.pallas_call(
        functools.partial(_attn_prep_kernel, geo=geo, tb=tb),
        grid=(N // tb,),
        in_specs=[zs("g_q", 512), zs("g_k", 256), zs("g_v", 256), zs("x_q", 512), zs("x_k", 512), zs("x_v", 512),
                  vec, vec, vec, vec, tab, tab, tab, tab],
        out_specs=[row(512), row(256), row(256), row(1024), row(512), row(512)],
        out_shape=[jax.ShapeDtypeStruct((N, w), BF16) for w in (512, 256, 256, 1024, 512, 512)],
        compiler_params=_cp(("parallel",)),
        name="attn_prep",
    )(z, z, z, z, z, z, gq_g.reshape(1, LANE), gk_g.reshape(1, LANE),
      jnp.tile(xq_g, rep).reshape(1, LANE), jnp.tile(xk_g, rep).reshape(1, LANE), cg, sg, cx, sx)


def _flash_kernel(*refs, scale, tk, n_lat, mode, lam_init):
    refs = list(refs)
    q_ref = refs.pop(0)
    if n_lat:
        kl_ref, vl_ref = refs.pop(0), refs.pop(0)
    kc_ref, vc_ref = refs.pop(0), refs.pop(0)
    if mode == "diff":
        lam_ref, sg_ref = refs.pop(0), refs.pop(0)
    o_ref = refs.pop(0)
    tq = q_ref.shape[0]
    q = jnp.concatenate([q_ref[:, :LANE], q_ref[:, LANE:]], axis=0)
    c_exp = scale * math.log2(math.e)

    def ones_col(n):
        return jnp.where(lax.broadcasted_iota(jnp.int32, (n, LANE), 1) == 0, 1.0, 0.0).astype(BF16)

    def chunk(k, v, e0, carry):
        m, acc = carry
        s = lax.dot_general(q, k, (((1,), (1,)), ((), ())), preferred_element_type=F32)
        m_new = jnp.maximum(m, jnp.max(s, axis=-1, keepdims=True))
        p = jnp.exp2((s - m_new) * c_exp)
        alpha = jnp.exp2((m - m_new) * c_exp)
        v_aug = jnp.concatenate([v, e0], axis=1)
        acc = alpha * acc + jnp.dot(p.astype(BF16), v_aug, preferred_element_type=F32)
        return m_new, acc

    carry = (jnp.full((2 * tq, 1), -jnp.inf, F32), jnp.zeros((2 * tq, 2 * LANE), F32))
    if n_lat:
        e_lat = ones_col(tk)

        def body(j, carry):
            rows = pl.ds(pl.multiple_of(j * tk, tk), tk)
            return chunk(kl_ref[rows, :], vl_ref[rows, :], e_lat, carry)

        carry = lax.fori_loop(0, n_lat, body, carry, unroll=8 if n_lat % 8 == 0 else 1)
    _, acc = chunk(kc_ref[...], vc_ref[...], ones_col(kc_ref.shape[0]), carry)
    o = acc[:, :LANE] / acc[:, LANE:LANE + 1]
    outs = [o[:tq], o[tq:]]
    if mode == "gqa":
        o_ref[:, :LANE] = outs[0].astype(o_ref.dtype)
        o_ref[:, LANE:] = outs[1].astype(o_ref.dtype)
    else:
        dl = lam_ref[...]
        lam = (jnp.exp(jnp.sum(dl[0:1] * dl[1:2], axis=-1, keepdims=True))
               - jnp.exp(jnp.sum(dl[2:3] * dl[3:4], axis=-1, keepdims=True)) + lam_init)
        x = outs[0] - lam * outs[1]
        y = x * lax.rsqrt(jnp.mean(x * x, axis=-1, keepdims=True) + EPS) * sg_ref[...]
        o_ref[...] = (y * (1.0 - lam_init)).astype(o_ref.dtype)


def attention(q, k, v, geo, *, n_kv, d, mode, ctx_queries, lam=None, sub_g=None, lam_init=0.0):
    B, S, L = geo.B, geo.S, geo.L
    ctx0 = (B * S) // L
    if ctx_queries:
        tq, nq, n_lat, tk = L, 1, 0, L
        qrow = lambda b, h, i: ctx0 + b
        orow = lambda b, h, i: b
    else:
        tq = _pick(S, (256, 128, 64))
        nq = S // tq
        tk = _pick(S, (1024, 512, 256, 128, 64))
        n_lat = S // tk
        qrow = lambda b, h, i: b * nq + i
        orow = qrow
    n_out = B * tq * nq
    in_specs = [pl.BlockSpec((tq, 2 * LANE), lambda b, h, i: (qrow(b, h, i), h))]
    args = [q]
    if n_lat:
        in_specs += [pl.BlockSpec((S, LANE), lambda b, h, i: (b, h)), pl.BlockSpec((S, LANE), lambda b, h, i: (b, h))]
        args += [k, v]
    in_specs += [pl.BlockSpec((L, LANE), lambda b, h, i: (ctx0 + b, h)),
                 pl.BlockSpec((L, LANE), lambda b, h, i: (ctx0 + b, h))]
    args += [k, v]
    if mode == "diff":
        in_specs += [pl.BlockSpec((4, X_HD), lambda b, h, i: (0, 0)), pl.BlockSpec((1, LANE), lambda b, h, i: (0, 0))]
        args += [lam, sub_g.reshape(1, LANE)]
        ow = LANE
    else:
        ow = 2 * LANE
    return pl.pallas_call(
        functools.partial(_flash_kernel, scale=d ** -0.5, tk=tk, n_lat=n_lat, mode=mode, lam_init=lam_init),
        grid=(B, n_kv, nq),
        in_specs=in_specs,
        out_specs=pl.BlockSpec((tq, ow), lambda b, h, i: (orow(b, h, i), h)),
        out_shape=jax.ShapeDtypeStruct((n_out, n_kv * ow), BF16),
        compiler_params=_cp(("parallel", "parallel", "arbitrary")),
        name="attn_" + mode + ("_ctx" if ctx_queries else ""),
    )(*args)


def _router_kernel(t_ref, r_ref, o_ref):
    t = t_ref[...]
    th, tm_, _ = _split3(t)
    rh, rm, _ = _split3(r_ref[...])
    d = lambda a, b: jnp.dot(a, b, preferred_element_type=F32)
    logits = (d(tm_, rh) + d(th, rm)) + d(th, rh)
    lane = lax.broadcasted_iota(jnp.int32, logits.shape, 1).astype(F32)
    logits = jnp.where(lane < N_EXPERTS, logits, -jnp.inf)
    v1 = jnp.max(logits, axis=-1, keepdims=True)
    i1 = jnp.min(jnp.where(logits == v1, lane, float(LANE)), axis=-1, keepdims=True)
    rest = jnp.where(lane == i1, -jnp.inf, logits)
    v2 = jnp.max(rest, axis=-1, keepdims=True)
    i2 = jnp.min(jnp.where(rest == v2, lane, float(LANE)), axis=-1, keepdims=True)
    e2 = jnp.exp(v2 - v1)
    den = 1.0 + e2
    out = jnp.where(lane == 0, i1, 0.0)
    out = jnp.where(lane == 1, i2, out)
    out = jnp.where(lane == 2, 1.0 / den, out)
    out = jnp.where(lane == 3, e2 / den, out)
    o_ref[...] = out


def router(t, r, tm):
    N, D = t.shape
    rp = jnp.zeros((D, LANE), F32).at[:, :N_EXPERTS].set(r)
    return pl.pallas_call(
        _router_kernel,
        grid=(N // tm,),
        in_specs=[pl.BlockSpec((tm, D), lambda m: (m, 0)), pl.BlockSpec((D, LANE), lambda m: (0, 0))],
        out_specs=pl.BlockSpec((tm, LANE), lambda m: (m, 0)),
        out_shape=jax.ShapeDtypeStruct((N, LANE), F32),
        compiler_params=_cp(("parallel",)),
        name="router",
    )(t, rp)


def _row_copy(src_hbm, dst_vmem, sem, src_row, dst_row):
    return pltpu.make_async_copy(src_hbm.at[pl.ds(src_row, 1), :], dst_vmem.at[pl.ds(dst_row, 1), :], sem)


def _gather_issue(src_hbm, idx_smem, dst_vmem, sem, n):
    def issue(r, c):
        _row_copy(src_hbm, dst_vmem, sem, idx_smem[0, 0, r], r).start()
        return c

    lax.fori_loop(0, n, issue, 0, unroll=8)


def _gather_drain(src_hbm, dst_vmem, sem, n):
    def drain(r, c):
        _row_copy(src_hbm, dst_vmem, sem, 0, r).wait()
        return c

    lax.fori_loop(0, n, drain, 0, unroll=8)


def _load_idx(idx_vmem, idx_smem, sem):
    cp = pltpu.make_async_copy(idx_vmem, idx_smem, sem)
    cp.start()
    cp.wait()


def _gather_kernel(idx_ref, src_ref, o_ref, idx_smem, buf, sem_i, sem_r, *, tg):
    _load_idx(idx_ref, idx_smem, sem_i)
    _gather_issue(src_ref, idx_smem, buf, sem_r, tg)
    _gather_drain(src_ref, buf, sem_r, tg)
    o_ref[...] = buf[...].astype(o_ref.dtype)


def gather_rows(src, idx, tg, out_dtype):
    P = idx.shape[0]
    D = src.shape[1]
    return pl.pallas_call(
        functools.partial(_gather_kernel, tg=tg),
        grid=(P // tg,),
        in_specs=[pl.BlockSpec((1, 1, tg), lambda i: (i, 0, 0)), pl.BlockSpec(memory_space=pl.ANY)],
        out_specs=pl.BlockSpec((tg, D), lambda i: (i, 0)),
        out_shape=jax.ShapeDtypeStruct((P, D), out_dtype),
        scratch_shapes=[pltpu.SMEM((1, 1, tg), jnp.int32), pltpu.VMEM((tg, D), src.dtype),
                        pltpu.SemaphoreType.DMA(()), pltpu.SemaphoreType.DMA(())],
        compiler_params=_cp(("arbitrary",)),
        name="moe_gather",
    )(idx.reshape(P // tg, 1, tg), src)


def _combine_kernel(p0_ref, p1_ref, ys_ref, route_ref, x_ref, gate_ref, o_ref, s0, s1, b0, b1, sem_i, sem_r0, sem_r1,
                    *, tg):
    _load_idx(p0_ref, s0, sem_i)
    _load_idx(p1_ref, s1, sem_i)
    _gather_issue(ys_ref, s0, b0, sem_r0, tg)
    _gather_issue(ys_ref, s1, b1, sem_r1, tg)
    _gather_drain(ys_ref, b0, sem_r0, tg)
    _gather_drain(ys_ref, b1, sem_r1, tg)
    w0 = route_ref[:, TOP_K:TOP_K + 1]
    w1 = route_ref[:, TOP_K + 1:TOP_K + 2]
    o_ref[...] = x_ref[...] + gate_ref[...] * (b0[...] * w0 + b1[...] * w1)


def moe_combine(ys, pos0, pos1, route, x, mod4, geo, gate_idx, rows, tg):
    D = geo.D
    idx_spec = pl.BlockSpec((1, 1, tg), lambda i: (i, 0, 0))
    return pl.pallas_call(
        functools.partial(_combine_kernel, tg=tg),
        grid=(rows // tg,),
        in_specs=[idx_spec, idx_spec, pl.BlockSpec(memory_space=pl.ANY),
                  pl.BlockSpec((tg, LANE), lambda i: (i, 0)),
                  pl.BlockSpec((tg, D), lambda i: (i, 0)),
                  pl.BlockSpec((None, None, 1, D), lambda i: (geo.group(i * tg), gate_idx, 0, 0))],
        out_specs=pl.BlockSpec((tg, D), lambda i: (i, 0)),
        out_shape=jax.ShapeDtypeStruct((rows, D), F32),
        scratch_shapes=[pltpu.SMEM((1, 1, tg), jnp.int32), pltpu.SMEM((1, 1, tg), jnp.int32),
                        pltpu.VMEM((tg, D), F32), pltpu.VMEM((tg, D), F32),
                        pltpu.SemaphoreType.DMA(()), pltpu.SemaphoreType.DMA(()), pltpu.SemaphoreType.DMA(())],
        compiler_params=_cp(("arbitrary",)),
        name="moe_combine",
    )(pos0.reshape(rows // tg, 1, tg), pos1.reshape(rows // tg, 1, tg), ys, route, x, mod4)


def moe_ffn(h2, x, mod4, geo, gate_idx, r_w, w1, w3, w2, rows):
    D = geo.D
    E = w1.shape[0]
    F = w1.shape[2]
    tr = _pick(rows, (512, 256, 128, 64))
    route = router(h2, r_w, tr)
    e_flat = route[:, :TOP_K].astype(jnp.int32).reshape(-1)
    n_assign = rows * TOP_K
    tm = _pick(rows, (512, 256, 128, 64))
    onehot = (e_flat[:, None] == jnp.arange(E, dtype=jnp.int32)[None, :]).astype(jnp.int32)
    csum = jnp.cumsum(onehot, axis=0)
    rank = jnp.sum(csum * onehot, axis=1) - 1
    counts = csum[-1]
    padded = (counts + tm - 1) // tm * tm
    pad_end = jnp.cumsum(padded)
    pad_start = pad_end - padded
    pos = pad_start[e_flat] + rank
    P = n_assign + E * tm
    n_tiles = P // tm
    tok_flat = jnp.repeat(jnp.arange(rows, dtype=jnp.int32), TOP_K)
    buf_tok = jnp.zeros((P,), jnp.int32).at[pos].set(tok_flat)
    tile_e = jnp.minimum(jnp.searchsorted(pad_end, jnp.arange(n_tiles, dtype=jnp.int32) * tm, side="right"),
                         E - 1).astype(jnp.int32)
    n_used = (pad_end[-1] // tm).astype(jnp.int32).reshape(1)

    tg = _pick(rows, (256, 128, 64))
    xs = gather_rows(h2, buf_tok, tg, BF16)
    tn1 = _pick(F, (1024, 512, 256, 128))
    hb = grouped_mm(xs, [w1, w3], tile_e, n_used, tm=tm, tn=tn1, tk=D, epi="swiglu", out_dtype=BF16, name="moe_up")
    tn2 = _pick(D, (512, 256, 128))
    ys = grouped_mm(hb, [w2], tile_e, n_used, tm=tm, tn=tn2, tk=F, epi="none", out_dtype=F32, name="moe_down")
    pos2 = pos.reshape(rows, TOP_K)
    return moe_combine(ys, pos2[:, 0], pos2[:, 1], route, x, mod4, geo, gate_idx, rows, tg)


def _pack_w_in(w):
    D = w.shape[0]
    cols, at = [], 0
    for piece, off, width in sorted(_Z.values(), key=lambda v: v[1]):
        assert off == at
        lo, hi = _REF_OFF[piece], _REF_OFF[piece + 1]
        cols.append(w[:, lo:hi].astype(BF16))
        if width > hi - lo:
            cols.append(jnp.zeros((D, width - (hi - lo)), BF16))
        at += width
    assert at == Z_W
    return jnp.concatenate(cols, axis=1)


def kernel(x, c, ctx, c_ctx, ada_w, ada_b, norm1_g, norm2_g, w_in, mlstm_conv_w, mlstm_conv_b, mlstm_gate_b, mlstm_norm_g, conv_dw_w, conv_dw_b, conv_ln_g, conv_ln_b, gqa_q_norm_g, gqa_k_norm_g, diff_q_norm_g, diff_k_norm_g, diff_lambda, diff_subln_g, merge_gate_w, merge_gate_b, branch_w, out_w, ffn_w1, ffn_w3, ffn_w2, moe_router, moe_w1, moe_w3, moe_w2):
    B, S, D = x.shape
    L = ctx.shape[1]
    depth = ada_w.shape[0]
    geo = Geo(B, S, L, D)
    N, BS = geo.N, geo.BS
    assert S % GRID_W == 0 and S % M_CHUNK == 0 and L % M_CHUNK == 0 and BS % L == 0
    tb = _pick(math.gcd(S, L), (256, 128, 64))
    tm = _pick(math.gcd(S, B * L), (1024, 512, 256, 128))

    xt = jnp.concatenate([x.reshape(BS, D), ctx.reshape(B * L, D)], axis=0)
    cc = jnp.zeros((8, D), F32).at[:B].set(c).at[B].set(c_ctx)

    for l in range(depth):
        need_ctx = l < depth - 1
        rows = N if need_ctx else BS
        lam_init = 0.8 - 0.6 * math.exp(-0.3 * l)
        mod = adaln(cc, ada_w[l], ada_b[l])
        mod4 = mod[:B + 1].reshape(B + 1, 6, 1, D)

        h = norm_mod(xt, norm1_g[l], mod4, geo, 0, 1, N, BF16)
        z = dense_mm(h, [_pack_w_in(w_in[l])], rows=N, tm=tm, tn=_pick(Z_W, (640, 128)), tk=D, order="mn",
                     epi="none", out_dtype=F32, name="w_in")

        mq, mk = mlstm_qk(z, mlstm_conv_w[l], mlstm_conv_b[l], geo, tb)
        bc, cg = mlstm_gates(z, mlstm_gate_b[l], geo, tb)
        lanes = np.array([_gate_lane(d, hh) for d in range(2) for hh in range(M_HEADS)])
        cr = cg[:, lanes].reshape(N // M_CHUNK, M_CHUNK, 2 * M_HEADS).transpose(0, 2, 1)
        hf, hbw = mlstm_scan(mq, mk, z, bc, cg, cr, geo, tb)
        a_br = mlstm_out(hf, hbw, z, mlstm_norm_g[l], rows, tb)

        b_br = conformer(z, conv_dw_w[l], conv_dw_b[l], conv_ln_g[l], conv_ln_b[l], geo, tb)

        gq, gk, gv, xq, xk, xv = attn_prep(z, gqa_q_norm_g[l], gqa_k_norm_g[l], diff_q_norm_g[l], diff_k_norm_g[l],
                                           geo, tb)
        dkw = dict(lam=diff_lambda[l], sub_g=diff_subln_g[l], lam_init=lam_init)
        c_br = attention(gq, gk, gv, geo, n_kv=G_KV, d=G_HD, mode="gqa", ctx_queries=False)
        d_br = attention(xq, xk, xv, geo, n_kv=X_HEADS, d=X_HD, mode="diff", ctx_queries=False, **dkw)
        if need_ctx:
            c_ctx_o = attention(gq, gk, gv, geo, n_kv=G_KV, d=G_HD, mode="gqa", ctx_queries=True)
            d_ctx_o = attention(xq, xk, xv, geo, n_kv=X_HEADS, d=X_HD, mode="diff", ctx_queries=True, **dkw)
            c_br = jnp.concatenate([c_br, c_ctx_o], axis=0)
            d_br = jnp.concatenate([d_br, d_ctx_o], axis=0)

        tn = _pick(D, (512, 256, 128))
        y = merge(h, [a_br, b_br, c_br, d_br], merge_gate_w[l].astype(BF16), merge_gate_b[l].reshape(4, 1, D),
                  branch_w[l].astype(BF16), rows=rows, tm=tm, tn=tn)
        xt = dense_mm(y, [out_w[l].astype(BF16)], rows=rows, tm=tm, tn=tn, tk=D, order="mn", epi="resid",
                      out_dtype=F32, geo=geo, res=xt, mod4=mod4, gate_idx=2, name="out_proj")

        i = l // 2
        if l % 2 == 0:
            h2 = norm_mod(xt, norm2_g[l], mod4, geo, 3, 4, rows, BF16)
            F = ffn_w1.shape[2]
            tnf = _pick(F, (512, 256, 128))
            hb = dense_mm(h2, [ffn_w1[i].astype(BF16), ffn_w3[i].astype(BF16)], rows=rows, tm=tm, tn=tnf, tk=D,
                          order="mn", epi="swiglu", out_dtype=BF16, name="ffn_up")
            xt = dense_mm(hb, [ffn_w2[i].astype(BF16)], rows=rows, tm=tm, tn=_pick(D, (1024, 512, 256, 128)),
                          tk=_pick(F, (512, 256, 128)),
                          order="mn", epi="resid", out_dtype=F32, geo=geo, res=xt, mod4=mod4, gate_idx=5,
                          name="ffn_down")
        else:
            h2 = norm_mod(xt, norm2_g[l], mod4, geo, 3, 4, rows, F32)
            xt = moe_ffn(h2, xt, mod4, geo, 5, moe_router[i], moe_w1[i], moe_w3[i], moe_w2[i].astype(BF16), rows)
    return xt[:BS].reshape(B, S, D)
```

```python
import functools
import math

import numpy as np
import jax
import jax.numpy as jnp
from jax import lax
from jax.experimental import pallas as pl
from jax.experimental.pallas import tpu as pltpu

F32 = jnp.float32
BF16 = jnp.bfloat16

GRID_W = 64
M_HEADS, M_HD, M_CHUNK, M_CONV = 4, 128, 64, 3
M_W = M_HEADS * M_HD
C_W, C_K = 512, 31
G_HEADS, G_KV, G_HD = 4, 2, 128
X_HEADS, X_HD = 4, 64
X_VD = 2 * X_HD
ROPE_THETA = 10000.0
N_EXPERTS, TOP_K = 8, 2
EPS = 1e-6
LANE = 128

_REF_SPLITS = (2 * M_W, M_W, M_W, 4 * M_HEADS, 2 * C_W, G_HEADS * G_HD, G_KV * G_HD, G_KV * G_HD,
               X_HEADS * 2 * X_HD, X_HEADS * 2 * X_HD, X_HEADS * X_VD)
_REF_OFF = np.concatenate([[0], np.cumsum(_REF_SPLITS)]).tolist()
_Z = {
    "m_qk": (0, 0, 1024), "c_glu": (4, 1024, 1024), "m_v": (1, 2048, 512), "m_o": (2, 2560, 512),
    "g_q": (5, 3072, 512), "x_q": (8, 3584, 512), "x_k": (9, 4096, 512), "x_v": (10, 4608, 512),
    "g_k": (6, 5120, 256), "g_v": (7, 5376, 256), "m_g": (3, 5632, 128),
}
Z_W = 5760

_VMEM_LIMIT = 56 * 1024 * 1024


def _cp(sem, vmem=_VMEM_LIMIT):
    return pltpu.CompilerParams(dimension_semantics=sem, vmem_limit_bytes=vmem)


def _pick(n, cands):
    for c in cands:
        if n % c == 0:
            return c
    raise ValueError(f"no tile for {n} in {cands}")


def _sigmoid(x):
    return 1.0 / (1.0 + jnp.exp(-x))


def _silu(x):
    return x * _sigmoid(x)


def _adaln_kernel(c_ref, w_ref, b_ref, o_ref):
    s = _silu(c_ref[...]).astype(BF16)
    o_ref[...] = jnp.dot(s, w_ref[...].astype(BF16), preferred_element_type=F32) + b_ref[...]


def adaln(cc, w, b):
    R, D = cc.shape
    N = w.shape[1]
    tn = _pick(N, (512, 256, 128))
    return pl.pallas_call(
        _adaln_kernel,
        grid=(N // tn,),
        in_specs=[pl.BlockSpec((R, D), lambda n: (0, 0)),
                  pl.BlockSpec((D, tn), lambda n: (0, n)),
                  pl.BlockSpec((1, tn), lambda n: (0, n))],
        out_specs=pl.BlockSpec((R, tn), lambda n: (0, n)),
        out_shape=jax.ShapeDtypeStruct((R, N), F32),
        compiler_params=_cp(("parallel",)),
        name="adaln",
    )(cc, w, b.reshape(1, N))


class Geo:
    def __init__(self, B, S, L, D):
        self.B, self.S, self.L, self.D = B, S, L, D
        self.BS = B * S
        self.N = B * S + B * L

    def group(self, row0):
        return jnp.minimum(row0 // self.S, self.B)


def _norm_mod_kernel(x_ref, g_ref, sh_ref, sc_ref, o_ref):
    x = x_ref[...]
    y = x * lax.rsqrt(jnp.mean(x * x, axis=-1, keepdims=True) + EPS)
    y = y * g_ref[...]
    o_ref[...] = (y * (1.0 + sc_ref[...]) + sh_ref[...]).astype(o_ref.dtype)


def norm_mod(x, g, mod4, geo, i_shift, i_scale, rows, out_dtype):
    D = geo.D
    tm = _pick(math.gcd(geo.S, geo.B * geo.L), (256, 128, 64))
    return pl.pallas_call(
        _norm_mod_kernel,
        grid=(rows // tm,),
        in_specs=[pl.BlockSpec((tm, D), lambda m: (m, 0)),
                  pl.BlockSpec((1, D), lambda m: (0, 0)),
                  pl.BlockSpec((None, None, 1, D), lambda m: (geo.group(m * tm), i_shift, 0, 0)),
                  pl.BlockSpec((None, None, 1, D), lambda m: (geo.group(m * tm), i_scale, 0, 0))],
        out_specs=pl.BlockSpec((tm, D), lambda m: (m, 0)),
        out_shape=jax.ShapeDtypeStruct((rows, D), out_dtype),
        compiler_params=_cp(("parallel",)),
        name="norm_mod",
    )(x, g.reshape(1, D), mod4, mod4)


def _bf16(x):
    return x if x.dtype == BF16 else x.astype(BF16)


def _mm_body(a_ref, w_refs, o_ref, acc_refs, nk, finalize):
    k = pl.program_id(2)
    a = _bf16(a_ref[...])
    parts = [jnp.dot(a, _bf16(w[...]), preferred_element_type=F32) for w in w_refs]
    if nk == 1:
        finalize(parts)
        return

    @pl.when(k == 0)
    def _():
        for acc, p in zip(acc_refs, parts):
            acc[...] = p

    @pl.when(k > 0)
    def _():
        for acc, p in zip(acc_refs, parts):
            acc[...] += p

    @pl.when(k == nk - 1)
    def _():
        finalize([acc[...] for acc in acc_refs])


def _dense_mm_kernel(*refs, n_w, nk, epi):
    refs = list(refs)
    a_ref = refs.pop(0)
    w_refs = [refs.pop(0) for _ in range(n_w)]
    extra = []
    if epi == "resid":
        extra = [refs.pop(0), refs.pop(0)]
    o_ref = refs.pop(0)

    def finalize(parts):
        if epi == "none":
            o_ref[...] = parts[0].astype(o_ref.dtype)
        elif epi == "swiglu":
            o_ref[...] = (_silu(parts[0]) * parts[1]).astype(o_ref.dtype)
        else:
            o_ref[...] = (extra[0][...] + extra[1][...] * parts[0]).astype(o_ref.dtype)

    _mm_body(a_ref, w_refs, o_ref, refs, nk, finalize)


def dense_mm(a, ws, *, rows, tm, tn, tk, order, epi, out_dtype, geo=None, res=None, mod4=None, gate_idx=None, name="mm"):
    K = a.shape[1]
    N = ws[0].shape[1]
    nm, nn, nk = rows // tm, N // tn, K // tk
    assert rows % tm == 0 and N % tn == 0 and K % tk == 0
    if order == "mn":
        grid = (nm, nn, nk)
        mi = lambda g0, g1: g0
        ni = lambda g0, g1: g1
    else:
        grid = (nn, nm, nk)
        mi = lambda g0, g1: g1
        ni = lambda g0, g1: g0
    in_specs = [pl.BlockSpec((tm, tk), lambda g0, g1, k: (mi(g0, g1), k))]
    in_specs += [pl.BlockSpec((tk, tn), lambda g0, g1, k: (k, ni(g0, g1))) for _ in ws]
    args = [a, *ws]
    if epi == "resid":
        in_specs.append(pl.BlockSpec((tm, tn), lambda g0, g1, k: (mi(g0, g1), ni(g0, g1))))
        in_specs.append(pl.BlockSpec((None, None, 1, tn),
                                     lambda g0, g1, k: (geo.group(mi(g0, g1) * tm), gate_idx, 0, ni(g0, g1))))
        args += [res, mod4]
    scratch = [pltpu.VMEM((tm, tn), F32) for _ in ws] if nk > 1 else []
    return pl.pallas_call(
        functools.partial(_dense_mm_kernel, n_w=len(ws), nk=nk, epi=epi),
        grid=grid,
        in_specs=in_specs,
        out_specs=pl.BlockSpec((tm, tn), lambda g0, g1, k: (mi(g0, g1), ni(g0, g1))),
        out_shape=jax.ShapeDtypeStruct((rows, N), out_dtype),
        scratch_shapes=scratch,
        compiler_params=_cp(("parallel", "parallel", "arbitrary")),
        name=name,
    )(*args)


def _grouped_mm_kernel(te_ref, nu_ref, *refs, n_w, nk, epi):
    refs = list(refs)
    a_ref = refs.pop(0)
    w_refs = [refs.pop(0) for _ in range(n_w)]
    o_ref = refs.pop(0)
    m = pl.program_id(1)
    k = pl.program_id(2)

    def finalize(parts):
        if epi == "swiglu":
            o_ref[...] = (_silu(parts[0]) * parts[1]).astype(o_ref.dtype)
        else:
            o_ref[...] = parts[0].astype(o_ref.dtype)

    @pl.when(m < nu_ref[0])
    def _():
        _mm_body(a_ref, w_refs, o_ref, refs, nk, finalize)

    @pl.when(jnp.logical_and(m >= nu_ref[0], k == nk - 1))
    def _():
        o_ref[...] = jnp.zeros_like(o_ref)


def grouped_mm(a, ws, tile_e, n_used, *, tm, tn, tk, epi, out_dtype, name="gmm"):
    P, K = a.shape
    N = ws[0].shape[2]
    nm, nn, nk = P // tm, N // tn, K // tk
    assert P % tm == 0 and N % tn == 0 and K % tk == 0

    def mc(m, nu):
        return jnp.minimum(m, nu[0] - 1)

    in_specs = [pl.BlockSpec((tm, tk), lambda n, m, k, te, nu: (mc(m, nu), k))]
    in_specs += [pl.BlockSpec((None, tk, tn), lambda n, m, k, te, nu: (te[mc(m, nu)], k, n)) for _ in ws]
    args = [a, *ws]
    scratch = [pltpu.VMEM((tm, tn), F32) for _ in ws] if nk > 1 else []
    return pl.pallas_call(
        functools.partial(_grouped_mm_kernel, n_w=len(ws), nk=nk, epi=epi),
        grid_spec=pltpu.PrefetchScalarGridSpec(
            num_scalar_prefetch=2,
            grid=(nn, nm, nk),
            in_specs=in_specs,
            out_specs=pl.BlockSpec((tm, tn), lambda n, m, k, te, nu: (m, n)),
            scratch_shapes=scratch),
        out_shape=jax.ShapeDtypeStruct((P, N), out_dtype),
        compiler_params=_cp(("parallel", "arbitrary", "arbitrary")),
        name=name,
    )(tile_e, n_used, *args)


def _merge_kernel(h_ref, wg_ref, bg_ref, b0_ref, b1_ref, b2_ref, b3_ref, wb_ref, o_ref):
    h = h_ref[...]
    acc = None
    for j, br in enumerate((b0_ref, b1_ref, b2_ref, b3_ref)):
        gate = _sigmoid(jnp.dot(h, wg_ref[j], preferred_element_type=F32) + bg_ref[j])
        term = gate * jnp.dot(br[...], wb_ref[j], preferred_element_type=F32)
        acc = term if acc is None else acc + term
    o_ref[...] = acc.astype(o_ref.dtype)


def merge(h, branches, wg, bg, wb, *, rows, tm, tn):
    D = h.shape[1]
    BW = branches[0].shape[1]
    nb = len(branches)
    br_spec = pl.BlockSpec((tm, BW), lambda m, n: (m, 0))
    return pl.pallas_call(
        _merge_kernel,
        grid=(rows // tm, D // tn),
        in_specs=[pl.BlockSpec((tm, D), lambda m, n: (m, 0)),
                  pl.BlockSpec((nb, D, tn), lambda m, n: (0, 0, n)),
                  pl.BlockSpec((nb, 1, tn), lambda m, n: (0, 0, n)),
                  br_spec, br_spec, br_spec, br_spec,
                  pl.BlockSpec((nb, BW, tn), lambda m, n: (0, 0, n))],
        out_specs=pl.BlockSpec((tm, tn), lambda m, n: (m, n)),
        out_shape=jax.ShapeDtypeStruct((rows, D), BF16),
        compiler_params=_cp(("parallel", "parallel")),
        name="merge",
    )(h, wg, bg, *branches, wb)


def _seq_flags(geo, tb):
    row0 = pl.program_id(0) * tb
    lat = row0 < geo.BS
    rel = row0 - geo.BS
    start = jnp.where(lat, row0 % geo.S == 0, rel % geo.L == 0)
    end = jnp.where(lat, (row0 + tb) % geo.S == 0, (rel + tb) % geo.L == 0)
    return start, end


def _halo_specs(geo, tb, hb, width, col_block):
    r = tb // hb
    last = geo.N // hb - 1
    return [pl.BlockSpec((tb, width), lambda i: (i, col_block)),
            pl.BlockSpec((hb, width), lambda i: (jnp.maximum(i * r - 1, 0), col_block)),
            pl.BlockSpec((hb, width), lambda i: (jnp.minimum((i + 1) * r, last), col_block))]


def _conv_rows(u_ref, w_ref, base, r0, rows, K):
    acc = None
    for j in range(K):
        t = w_ref[j:j + 1, :] * u_ref[pl.ds(base + r0 + j, rows), :]
        acc = t if acc is None else acc + t
    return acc


_C_HB = 16


def _conformer_kernel(main_ref, prev_ref, next_ref, w_ref, b_ref, lg_ref, lb_ref, o_ref, u_ref, *, geo, tb):
    start, end = _seq_flags(geo, tb)

    def glu(blk):
        return blk[:, :C_W] * _sigmoid(blk[:, C_W:])

    u_ref[pl.ds(_C_HB, tb), :] = glu(main_ref[...])
    u_ref[pl.ds(0, _C_HB), :] = jnp.where(start, 0.0, glu(prev_ref[...]))
    u_ref[pl.ds(_C_HB + tb, _C_HB), :] = jnp.where(end, 0.0, glu(next_ref[...]))
    rc = 32
    for r0 in range(0, tb, rc):
        u = _conv_rows(u_ref, w_ref, _C_HB - C_K // 2, r0, rc, C_K) + b_ref[...]
        mu = jnp.mean(u, axis=-1, keepdims=True)
        xc = u - mu
        y = xc * lax.rsqrt(jnp.mean(xc * xc, axis=-1, keepdims=True) + EPS)
        y = y * lg_ref[...] + lb_ref[...]
        o_ref[pl.ds(r0, rc), :] = _silu(y).astype(o_ref.dtype)


def conformer(z, w, b, lg, lb, geo, tb):
    cb = _Z["c_glu"][1] // (2 * C_W)
    return pl.pallas_call(
        functools.partial(_conformer_kernel, geo=geo, tb=tb),
        grid=(geo.N // tb,),
        in_specs=_halo_specs(geo, tb, _C_HB, 2 * C_W, cb) + [
            pl.BlockSpec((C_K, C_W), lambda i: (0, 0)),
            pl.BlockSpec((1, C_W), lambda i: (0, 0)),
            pl.BlockSpec((1, C_W), lambda i: (0, 0)),
            pl.BlockSpec((1, C_W), lambda i: (0, 0))],
        out_specs=pl.BlockSpec((tb, C_W), lambda i: (i, 0)),
        out_shape=jax.ShapeDtypeStruct((geo.N, C_W), BF16),
        scratch_shapes=[pltpu.VMEM((tb + 2 * _C_HB, C_W), F32)],
        compiler_params=_cp(("parallel",)),
        name="conformer",
    )(z, z, z, w, b.reshape(1, C_W), lg.reshape(1, C_W), lb.reshape(1, C_W))


_M_HB = 8


def _mlstm_qk_kernel(main_ref, prev_ref, next_ref, w_ref, b_ref, q_ref, k_ref, u_ref, *, geo, tb):
    start, end = _seq_flags(geo, tb)
    u_ref[pl.ds(_M_HB, tb), :] = main_ref[...]
    u_ref[pl.ds(0, _M_HB), :] = jnp.where(start, 0.0, prev_ref[...])
    u_ref[pl.ds(_M_HB + tb, _M_HB), :] = jnp.where(end, 0.0, next_ref[...])
    rc = 32
    for r0 in range(0, tb, rc):
        y = _silu(_conv_rows(u_ref, w_ref, _M_HB - M_CONV // 2, r0, rc, M_CONV) + b_ref[...])
        q_ref[pl.ds(r0, rc), :] = y[:, :M_W].astype(q_ref.dtype)
        k_ref[pl.ds(r0, rc), :] = (y[:, M_W:] * (M_HD ** -0.5)).astype(k_ref.dtype)


def mlstm_qk(z, w, b, geo, tb):
    cb = _Z["m_qk"][1] // (2 * M_W)
    return pl.pallas_call(
        functools.partial(_mlstm_qk_kernel, geo=geo, tb=tb),
        grid=(geo.N // tb,),
        in_specs=_halo_specs(geo, tb, _M_HB, 2 * M_W, cb) + [
            pl.BlockSpec((M_CONV, 2 * M_W), lambda i: (0, 0)),
            pl.BlockSpec((1, 2 * M_W), lambda i: (0, 0))],
        out_specs=[pl.BlockSpec((tb, M_W), lambda i: (i, 0)), pl.BlockSpec((tb, M_W), lambda i: (i, 0))],
        out_shape=[jax.ShapeDtypeStruct((geo.N, M_W), BF16), jax.ShapeDtypeStruct((geo.N, M_W), BF16)],
        scratch_shapes=[pltpu.VMEM((tb + 2 * _M_HB, 2 * M_W), F32)],
        compiler_params=_cp(("parallel",)),
        name="mlstm_qk",
    )(z, z, z, w, b.reshape(1, 2 * M_W))


def _split3(x):
    hi = x.astype(BF16)
    r1 = x - hi.astype(F32)
    mid = r1.astype(BF16)
    lo = (r1 - mid.astype(F32)).astype(BF16)
    return hi, mid, lo


def _log_sigmoid(x):
    return jnp.minimum(x, 0.0) - jnp.log(1.0 + jnp.exp(-jnp.abs(x)))


def _mlstm_gate_kernel(g_ref, b_ref, bc_ref, cc_ref, *, tb):
    raw = g_ref[...] + b_ref[...]
    lf = _log_sigmoid(raw)
    t = lax.broadcasted_iota(jnp.int32, (tb, tb), 0)
    s = lax.broadcasted_iota(jnp.int32, (tb, tb), 1)
    same = (t // M_CHUNK) == (s // M_CHUNK)
    lower = jnp.where(jnp.logical_and(same, s <= t), 1.0, 0.0).astype(BF16)
    upper = jnp.where(jnp.logical_and(same, s >= t), 1.0, 0.0).astype(BF16)
    hi, mid, lo = _split3(lf)

    def csum(tri):
        d = lambda p: jnp.dot(tri, p, preferred_element_type=F32)
        return (d(lo) + d(mid)) + d(hi)

    lane = lax.broadcasted_iota(jnp.int32, (tb, LANE), 1)
    b_all = jnp.where(lane < 2 * M_HEADS, csum(lower), csum(upper))
    i_sh = pltpu.roll(raw, M_HEADS, axis=1)
    bc_ref[...] = b_all
    cc_ref[...] = i_sh - b_all


def mlstm_gates(z, gate_b, geo, tb):
    cb = _Z["m_g"][1] // LANE
    gb = jnp.zeros((1, LANE), F32).at[0, :4 * M_HEADS].set(gate_b)
    spec = pl.BlockSpec((tb, LANE), lambda i: (i, 0))
    return pl.pallas_call(
        functools.partial(_mlstm_gate_kernel, tb=tb),
        grid=(geo.N // tb,),
        in_specs=[pl.BlockSpec((tb, LANE), lambda i: (i, cb)), pl.BlockSpec((1, LANE), lambda i: (0, 0))],
        out_specs=[spec, spec],
        out_shape=[jax.ShapeDtypeStruct((geo.N, LANE), F32)] * 2,
        compiler_params=_cp(("parallel",)),
        name="mlstm_gates",
    )(z, gb)


def _gate_lane(d, h):
    return 4 + 8 * d + h


def _mlstm_scan_kernel(*refs, tb):
    (qf, kf, vf, bcf, ccf, crf, qb, kb, vb, bcb, ccb, crb, hf_ref, hb_ref, ct_ref, m_ref) = refs
    i = pl.program_id(1)

    @pl.when(i == 0)
    def _():
        ct_ref[...] = jnp.zeros_like(ct_ref)
        m_ref[...] = jnp.zeros_like(m_ref)

    nc = tb // M_CHUNK
    t_i = lax.broadcasted_iota(jnp.int32, (M_CHUNK, M_CHUNK), 0)
    s_i = lax.broadcasted_iota(jnp.int32, (M_CHUNK, M_CHUNK), 1)
    e0 = jnp.where(lax.broadcasted_iota(jnp.int32, (M_CHUNK, M_HD), 1) == 0, 1.0, 0.0)
    dirs = ((0, qf, kf, vf, bcf, ccf, crf, hf_ref, s_i <= t_i), (1, qb, kb, vb, bcb, ccb, crb, hb_ref, s_i >= t_i))
    for d, q_ref, k_ref, v_ref, bc_ref, cc_ref, cr_ref, h_ref, mask in dirs:
        chunks = range(nc) if d == 0 else range(nc - 1, -1, -1)
        last = M_CHUNK - 1 if d == 0 else 0
        for h in range(M_HEADS):
            ln = _gate_lane(d, h)
            sid = d * M_HEADS + h
            m_prev = m_ref[sid, 0:1, 0:1]
            ct = ct_ref[sid]
            for c in chunks:
                rows = pl.ds(c * M_CHUNK, M_CHUNK)
                lanes = slice(h * M_HD, (h + 1) * M_HD)
                qc = q_ref[rows, lanes]
                kc = k_ref[rows, lanes]
                vc = v_ref[rows, lanes]
                col_r = cr_ref[c, sid:sid + 1, :]
                col_c = cc_ref[rows, ln:ln + 1]
                b_c = bc_ref[rows, ln:ln + 1]
                cm = jnp.max(jnp.where(mask, col_r, -jnp.inf), axis=-1, keepdims=True)
                mt = jnp.maximum(m_prev, cm)
                s = lax.dot_general(qc, kc, (((1,), (1,)), ((), ())), preferred_element_type=F32)
                w = jnp.where(mask, jnp.exp(col_r - mt), 0.0) * s
                v_aug = jnp.concatenate([vc, e0], axis=1)
                tot = jnp.exp(m_prev - mt) * jnp.dot(qc, ct.astype(BF16), preferred_element_type=F32)
                tot = tot + jnp.dot(w.astype(BF16), v_aug.astype(BF16), preferred_element_type=F32)
                den = jnp.maximum(jnp.abs(tot[:, M_HD:M_HD + 1]), jnp.exp(-b_c - mt))
                h_ref[rows, lanes] = tot[:, :M_HD] / den
                m_end = mt[last:last + 1, :]
                wg = jnp.exp(col_c - m_end)
                upd = lax.dot_general(kc, (wg * v_aug).astype(BF16), (((0,), (0,)), ((), ())),
                                      preferred_element_type=F32)
                ct = jnp.exp(m_prev - m_end) * ct + upd
                m_prev = b_c[last:last + 1, :] + m_end
            ct_ref[sid] = ct
            m_ref[sid] = jnp.broadcast_to(m_prev, m_ref.shape[1:])


def mlstm_scan(q, k, z, bc, cc, cr, geo, tb):
    B, S, L = geo.B, geo.S, geo.L
    nctx, nlat = L // tb, S // tb
    nblk = nctx + nlat
    vcb = _Z["m_v"][1] // M_W

    def blk(b, i, d):
        ctx_i = i if d == 0 else nctx - 1 - i
        lat_i = i - nctx if d == 0 else nlat - 1 - (i - nctx)
        return jnp.where(i < nctx, (B * S) // tb + b * nctx + ctx_i, b * nlat + lat_i)

    def specs(d):
        return [pl.BlockSpec((tb, M_W), lambda b, i: (blk(b, i, d), 0)),
                pl.BlockSpec((tb, M_W), lambda b, i: (blk(b, i, d), 0)),
                pl.BlockSpec((tb, M_W), lambda b, i: (blk(b, i, d), vcb)),
                pl.BlockSpec((tb, LANE), lambda b, i: (blk(b, i, d), 0)),
                pl.BlockSpec((tb, LANE), lambda b, i: (blk(b, i, d), 0)),
                pl.BlockSpec((tb // M_CHUNK, 2 * M_HEADS, M_CHUNK), lambda b, i: (blk(b, i, d), 0, 0))]

    out_f = pl.BlockSpec((tb, M_W), lambda b, i: (blk(b, i, 0), 0))
    out_b = pl.BlockSpec((tb, M_W), lambda b, i: (blk(b, i, 1), 0))
    return pl.pallas_call(
        functools.partial(_mlstm_scan_kernel, tb=tb),
        grid=(B, nblk),
        in_specs=specs(0) + specs(1),
        out_specs=[out_f, out_b],
        out_shape=[jax.ShapeDtypeStruct((geo.N, M_W), F32)] * 2,
        scratch_shapes=[pltpu.VMEM((2 * M_HEADS, M_HD, 2 * M_HD), F32), pltpu.VMEM((2 * M_HEADS, 8, LANE), F32)],
        compiler_params=_cp(("parallel", "arbitrary")),
        name="mlstm_scan",
    )(q, k, z, bc, cc, cr, q, k, z, bc, cc, cr)


def _mlstm_out_kernel(hf_ref, hb_ref, o_ref, g_ref, a_ref):
    hsum = hf_ref[...] + hb_ref[...]
    gate = _sigmoid(o_ref[...])
    for h in range(M_HEADS):
        lanes = slice(h * M_HD, (h + 1) * M_HD)
        x = hsum[:, lanes]
        y = x * lax.rsqrt(jnp.mean(x * x, axis=-1, keepdims=True) + EPS) * g_ref[:, lanes]
        a_ref[:, lanes] = (y * gate[:, lanes]).astype(a_ref.dtype)


def mlstm_out(hf, hb, z, g, rows, tb):
    ocb = _Z["m_o"][1] // M_W
    spec = pl.BlockSpec((tb, M_W), lambda i: (i, 0))
    return pl.pallas_call(
        _mlstm_out_kernel,
        grid=(rows // tb,),
        in_specs=[spec, spec, pl.BlockSpec((tb, M_W), lambda i: (i, ocb)), pl.BlockSpec((1, M_W), lambda i: (0, 0))],
        out_specs=spec,
        out_shape=jax.ShapeDtypeStruct((rows, M_W), BF16),
        compiler_params=_cp(("parallel",)),
        name="mlstm_out",
    )(hf, hb, z, g.reshape(1, M_W))


def _rope_tables(S, d):
    r = d // 4
    t = np.arange(S)
    freqs = ROPE_THETA ** (-np.arange(r, dtype=np.float32) / r)
    rows = (t // GRID_W).astype(np.float32)[:, None] * freqs
    cols = (t % GRID_W).astype(np.float32)[:, None] * freqs
    rows, cols = jnp.asarray(rows, F32), jnp.asarray(cols, F32)
    cos = jnp.concatenate([jnp.cos(rows), jnp.cos(rows), jnp.cos(cols), jnp.cos(cols)], axis=1)
    sin = jnp.concatenate([-jnp.sin(rows), jnp.sin(rows), -jnp.sin(cols), jnp.sin(cols)], axis=1)
    rep = LANE // d
    return jnp.tile(cos, (1, rep)), jnp.tile(sin, (1, rep))


def _norm_rope(x, g, cos, sin, is_lat, d):
    r = d // 4
    if d == LANE:
        ms = jnp.mean(x * x, axis=-1, keepdims=True)
    else:
        parts = [jnp.broadcast_to(jnp.mean(x[:, o:o + d] * x[:, o:o + d], axis=-1, keepdims=True), (x.shape[0], d))
                 for o in range(0, LANE, d)]
        ms = jnp.concatenate(parts, axis=1)
    y = x * lax.rsqrt(ms + EPS) * g
    lane = lax.broadcasted_iota(jnp.int32, y.shape, 1)
    first = (lane // r) % 2 == 0
    partner = jnp.where(first, pltpu.roll(y, LANE - r, axis=1), pltpu.roll(y, r, axis=1))
    return jnp.where(is_lat, y * cos + partner * sin, y)


def _attn_prep_kernel(gq_ref, gk_ref, gv_ref, xq_ref, xk_ref, xv_ref, gqg_ref, gkg_ref, xqg_ref, xkg_ref,
                      cg_ref, sg_ref, cx_ref, sx_ref, oq_ref, ok_ref, ov_ref, oxq_ref, oxk_ref, oxv_ref, *, geo, tb):
    is_lat = pl.program_id(0) * tb < geo.BS
    cg, sg, cx, sx = cg_ref[...], sg_ref[...], cx_ref[...], sx_ref[...]
    for h in range(G_HEADS):
        ls = slice(h * LANE, (h + 1) * LANE)
        oq_ref[:, ls] = _norm_rope(gq_ref[:, ls], gqg_ref[...], cg, sg, is_lat, G_HD).astype(BF16)
    for h in range(G_KV):
        ls = slice(h * LANE, (h + 1) * LANE)
        ok_ref[:, ls] = _norm_rope(gk_ref[:, ls], gkg_ref[...], cg, sg, is_lat, G_HD).astype(BF16)
    ov_ref[...] = gv_ref[...].astype(BF16)
    lane = lax.broadcasted_iota(jnp.int32, (tb, LANE), 1)
    for h in range(X_HEADS):
        ls = slice(h * LANE, (h + 1) * LANE)
        q = _norm_rope(xq_ref[:, ls], xqg_ref[...], cx, sx, is_lat, X_HD)
        oxq_ref[:, 2 * h * LANE:(2 * h + 1) * LANE] = jnp.where(lane < X_HD, q, 0.0).astype(BF16)
        oxq_ref[:, (2 * h + 1) * LANE:(2 * h + 2) * LANE] = jnp.where(lane >= X_HD, q, 0.0).astype(BF16)
        oxk_ref[:, ls] = _norm_rope(xk_ref[:, ls], xkg_ref[...], cx, sx, is_lat, X_HD).astype(BF16)
    oxv_ref[...] = xv_ref[...].astype(BF16)


def attn_prep(z, gq_g, gk_g, xq_g, xk_g, geo, tb):
    N = geo.N
    cg, sg = _rope_tables(geo.S, G_HD)
    cx, sx = _rope_tables(geo.S, X_HD)
    nlat_t = geo.S // tb

    def zs(name, width):
        cb = _Z[name][1] // width
        return pl.BlockSpec((tb, width), lambda i: (i, cb))

    tab = pl.BlockSpec((tb, LANE), lambda i: (i % nlat_t, 0))
    vec = pl.BlockSpec((1, LANE), lambda i: (0, 0))
    row = lambda w: pl.BlockSpec((tb, w), lambda i: (i, 0))
    rep = LANE // X_HD
    return pl.pallas_call(
        functools.partial(_attn_prep_kernel, geo=geo, tb=tb),
        grid=(N // tb,),
        in_specs=[zs("g_q", 512), zs("g_k", 256), zs("g_v", 256), zs("x_q", 512), zs("x_k", 512), zs("x_v", 512),
                  vec, vec, vec, vec, tab, tab, tab, tab],
        out_specs=[row(512), row(256), row(256), row(1024), row(512), row(512)],
        out_shape=[jax.ShapeDtypeStruct((N, w), BF16) for w in (512, 256, 256, 1024, 512, 512)],
        compiler_params=_cp(("parallel",)),
        name="attn_prep",
    )(z, z, z, z, z, z, gq_g.reshape(1, LANE), gk_g.reshape(1, LANE),
      jnp.tile(xq_g, rep).reshape(1, LANE), jnp.tile(xk_g, rep).reshape(1, LANE), cg, sg, cx, sx)


def _flash_kernel(*refs, scale, tk, n_lat, mode, lam_init):
    refs = list(refs)
    q_ref = refs.pop(0)
    if n_lat:
        kl_ref, vl_ref = refs.pop(0), refs.pop(0)
    kc_ref, vc_ref = refs.pop(0), refs.pop(0)
    if mode == "diff":
        lam_ref, sg_ref = refs.pop(0), refs.pop(0)
    o_ref = refs.pop(0)
    tq = q_ref.shape[0]
    q = jnp.concatenate([q_ref[:, :LANE], q_ref[:, LANE:]], axis=0)
    c_exp = scale * math.log2(math.e)

    def ones_col(n):
        return jnp.where(lax.broadcasted_iota(jnp.int32, (n, LANE), 1) == 0, 1.0, 0.0).astype(BF16)

    def chunk(k, v, e0, carry):
        m, acc = carry
        s = lax.dot_general(q, k, (((1,), (1,)), ((), ())), preferred_element_type=F32)
        m_new = jnp.maximum(m, jnp.max(s, axis=-1, keepdims=True))
        p = jnp.exp2((s - m_new) * c_exp)
        alpha = jnp.exp2((m - m_new) * c_exp)
        v_aug = jnp.concatenate([v, e0], axis=1)
        acc = alpha * acc + jnp.dot(p.astype(BF16), v_aug, preferred_element_type=F32)
        return m_new, acc

    carry = (jnp.full((2 * tq, 1), -jnp.inf, F32), jnp.zeros((2 * tq, 2 * LANE), F32))
    if n_lat:
        e_lat = ones_col(tk)

        def body(j, carry):
            rows = pl.ds(pl.multiple_of(j * tk, tk), tk)
            return chunk(kl_ref[rows, :], vl_ref[rows, :], e_lat, carry)

        carry = lax.fori_loop(0, n_lat, body, carry, unroll=8 if n_lat % 8 == 0 else 1)
    _, acc = chunk(kc_ref[...], vc_ref[...], ones_col(kc_ref.shape[0]), carry)
    o = acc[:, :LANE] / acc[:, LANE:LANE + 1]
    outs = [o[:tq], o[tq:]]
    if mode == "gqa":
        o_ref[:, :LANE] = outs[0].astype(o_ref.dtype)
        o_ref[:, LANE:] = outs[1].astype(o_ref.dtype)
    else:
        dl = lam_ref[...]
        lam = (jnp.exp(jnp.sum(dl[0:1] * dl[1:2], axis=-1, keepdims=True))
               - jnp.exp(jnp.sum(dl[2:3] * dl[3:4], axis=-1, keepdims=True)) + lam_init)
        x = outs[0] - lam * outs[1]
        y = x * lax.rsqrt(jnp.mean(x * x, axis=-1, keepdims=True) + EPS) * sg_ref[...]
        o_ref[...] = (y * (1.0 - lam_init)).astype(o_ref.dtype)


def attention(q, k, v, geo, *, n_kv, d, mode, ctx_queries, lam=None, sub_g=None, lam_init=0.0):
    B, S, L = geo.B, geo.S, geo.L
    ctx0 = (B * S) // L
    if ctx_queries:
        tq, nq, n_lat, tk = L, 1, 0, L
        qrow = lambda b, h, i: ctx0 + b
        orow = lambda b, h, i: b
    else:
        tq = _pick(S, (256, 128, 64))
        nq = S // tq
        tk = _pick(S, (1024, 512, 256, 128, 64))
        n_lat = S // tk
        qrow = lambda b, h, i: b * nq + i
        orow = qrow
    n_out = B * tq * nq
    in_specs = [pl.BlockSpec((tq, 2 * LANE), lambda b, h, i: (qrow(b, h, i), h))]
    args = [q]
    if n_lat:
        in_specs += [pl.BlockSpec((S, LANE), lambda b, h, i: (b, h)), pl.BlockSpec((S, LANE), lambda b, h, i: (b, h))]
        args += [k, v]
    in_specs += [pl.BlockSpec((L, LANE), lambda b, h, i: (ctx0 + b, h)),
                 pl.BlockSpec((L, LANE), lambda b, h, i: (ctx0 + b, h))]
    args += [k, v]
    if mode == "diff":
        in_specs += [pl.BlockSpec((4, X_HD), lambda b, h, i: (0, 0)), pl.BlockSpec((1, LANE), lambda b, h, i: (0, 0))]
        args += [lam, sub_g.reshape(1, LANE)]
        ow = LANE
    else:
        ow = 2 * LANE
    return pl.pallas_call(
        functools.partial(_flash_kernel, scale=d ** -0.5, tk=tk, n_lat=n_lat, mode=mode, lam_init=lam_init),
        grid=(B, n_kv, nq),
        in_specs=in_specs,
        out_specs=pl.BlockSpec((tq, ow), lambda b, h, i: (orow(b, h, i), h)),
        out_shape=jax.ShapeDtypeStruct((n_out, n_kv * ow), BF16),
        compiler_params=_cp(("parallel", "parallel", "arbitrary")),
        name="attn_" + mode + ("_ctx" if ctx_queries else ""),
    )(*args)


def _router_kernel(t_ref, r_ref, o_ref):
    t = t_ref[...]
    th, tm_, _ = _split3(t)
    rh, rm, _ = _split3(r_ref[...])
    d = lambda a, b: jnp.dot(a, b, preferred_element_type=F32)
    logits = (d(tm_, rh) + d(th, rm)) + d(th, rh)
    lane = lax.broadcasted_iota(jnp.int32, logits.shape, 1).astype(F32)
    logits = jnp.where(lane < N_EXPERTS, logits, -jnp.inf)
    v1 = jnp.max(logits, axis=-1, keepdims=True)
    i1 = jnp.min(jnp.where(logits == v1, lane, float(LANE)), axis=-1, keepdims=True)
    rest = jnp.where(lane == i1, -jnp.inf, logits)
    v2 = jnp.max(rest, axis=-1, keepdims=True)
    i2 = jnp.min(jnp.where(rest == v2, lane, float(LANE)), axis=-1, keepdims=True)
    e2 = jnp.exp(v2 - v1)
    den = 1.0 + e2
    out = jnp.where(lane == 0, i1, 0.0)
    out = jnp.where(lane == 1, i2, out)
    out = jnp.where(lane == 2, 1.0 / den, out)
    out = jnp.where(lane == 3, e2 / den, out)
    o_ref[...] = out


def router(t, r, tm):
    N, D = t.shape
    rp = jnp.zeros((D, LANE), F32).at[:, :N_EXPERTS].set(r)
    return pl.pallas_call(
        _router_kernel,
        grid=(N // tm,),
        in_specs=[pl.BlockSpec((tm, D), lambda m: (m, 0)), pl.BlockSpec((D, LANE), lambda m: (0, 0))],
        out_specs=pl.BlockSpec((tm, LANE), lambda m: (m, 0)),
        out_shape=jax.ShapeDtypeStruct((N, LANE), F32),
        compiler_params=_cp(("parallel",)),
        name="router",
    )(t, rp)


def _row_copy(src_hbm, dst_vmem, sem, src_row, dst_row):
    return pltpu.make_async_copy(src_hbm.at[pl.ds(src_row, 1), :], dst_vmem.at[pl.ds(dst_row, 1), :], sem)


def _gather_issue(src_hbm, idx_smem, dst_vmem, sem, n):
    def issue(r, c):
        _row_copy(src_hbm, dst_vmem, sem, idx_smem[0, 0, r], r).start()
        return c

    lax.fori_loop(0, n, issue, 0, unroll=8)


def _gather_drain(src_hbm, dst_vmem, sem, n):
    def drain(r, c):
        _row_copy(src_hbm, dst_vmem, sem, 0, r).wait()
        return c

    lax.fori_loop(0, n, drain, 0, unroll=8)


def _load_idx(idx_vmem, idx_smem, sem):
    cp = pltpu.make_async_copy(idx_vmem, idx_smem, sem)
    cp.start()
    cp.wait()


def _idx_specs(tg, nsteps):
    cur = pl.BlockSpec((1, 1, tg), lambda i: (i, 0, 0))
    nxt = pl.BlockSpec((1, 1, tg), lambda i: (jnp.minimum(i + 1, nsteps - 1), 0, 0))
    return cur, nxt


def _gather_kernel(idx_ref, idx_next_ref, src_ref, o_ref, idx_smem, buf, sem_i, sem_r, *, tg):
    i = pl.program_id(0)
    slot = i % 2

    def fetch(idx_vmem, s):
        _load_idx(idx_vmem, idx_smem.at[s], sem_i)
        _gather_issue(src_ref, idx_smem.at[s], buf.at[s], sem_r.at[s], tg)

    @pl.when(i == 0)
    def _():
        fetch(idx_ref, 0)

    @pl.when(i + 1 < pl.num_programs(0))
    def _():
        fetch(idx_next_ref, 1 - slot)

    _gather_drain(src_ref, buf.at[slot], sem_r.at[slot], tg)
    o_ref[...] = buf[slot].astype(o_ref.dtype)


def gather_rows(src, idx, tg, out_dtype):
    P = idx.shape[0]
    D = src.shape[1]
    nsteps = P // tg
    idx3 = idx.reshape(nsteps, 1, tg)
    return pl.pallas_call(
        functools.partial(_gather_kernel, tg=tg),
        grid=(nsteps,),
        in_specs=[*_idx_specs(tg, nsteps), pl.BlockSpec(memory_space=pl.ANY)],
        out_specs=pl.BlockSpec((tg, D), lambda i: (i, 0)),
        out_shape=jax.ShapeDtypeStruct((P, D), out_dtype),
        scratch_shapes=[pltpu.SMEM((2, 1, 1, tg), jnp.int32), pltpu.VMEM((2, tg, D), src.dtype),
                        pltpu.SemaphoreType.DMA(()), pltpu.SemaphoreType.DMA((2,))],
        compiler_params=_cp(("arbitrary",)),
        name="moe_gather",
    )(idx3, idx3, src)


def _combine_kernel(p0_ref, p0_next_ref, p1_ref, p1_next_ref, ys_ref, route_ref, x_ref, gate_ref, o_ref,
                    s0, s1, b0, b1, sem_i, sem_r0, sem_r1, *, tg):
    i = pl.program_id(0)
    slot = i % 2

    def fetch(p0_vmem, p1_vmem, s):
        _load_idx(p0_vmem, s0.at[s], sem_i)
        _load_idx(p1_vmem, s1.at[s], sem_i)
        _gather_issue(ys_ref, s0.at[s], b0.at[s], sem_r0.at[s], tg)
        _gather_issue(ys_ref, s1.at[s], b1.at[s], sem_r1.at[s], tg)

    @pl.when(i == 0)
    def _():
        fetch(p0_ref, p1_ref, 0)

    @pl.when(i + 1 < pl.num_programs(0))
    def _():
        fetch(p0_next_ref, p1_next_ref, 1 - slot)

    _gather_drain(ys_ref, b0.at[slot], sem_r0.at[slot], tg)
    _gather_drain(ys_ref, b1.at[slot], sem_r1.at[slot], tg)
    w0 = route_ref[:, TOP_K:TOP_K + 1]
    w1 = route_ref[:, TOP_K + 1:TOP_K + 2]
    o_ref[...] = x_ref[...] + gate_ref[...] * (b0[slot] * w0 + b1[slot] * w1)


def moe_combine(ys, pos0, pos1, route, x, mod4, geo, gate_idx, rows, tg):
    D = geo.D
    nsteps = rows // tg
    cur, nxt = _idx_specs(tg, nsteps)
    p0, p1 = pos0.reshape(nsteps, 1, tg), pos1.reshape(nsteps, 1, tg)
    return pl.pallas_call(
        functools.partial(_combine_kernel, tg=tg),
        grid=(nsteps,),
        in_specs=[cur, nxt, cur, nxt, pl.BlockSpec(memory_space=pl.ANY),
                  pl.BlockSpec((tg, LANE), lambda i: (i, 0)),
                  pl.BlockSpec((tg, D), lambda i: (i, 0)),
                  pl.BlockSpec((None, None, 1, D), lambda i: (geo.group(i * tg), gate_idx, 0, 0))],
        out_specs=pl.BlockSpec((tg, D), lambda i: (i, 0)),
        out_shape=jax.ShapeDtypeStruct((rows, D), F32),
        scratch_shapes=[pltpu.SMEM((2, 1, 1, tg), jnp.int32), pltpu.SMEM((2, 1, 1, tg), jnp.int32),
                        pltpu.VMEM((2, tg, D), F32), pltpu.VMEM((2, tg, D), F32),
                        pltpu.SemaphoreType.DMA(()), pltpu.SemaphoreType.DMA((2,)), pltpu.SemaphoreType.DMA((2,))],
        compiler_params=_cp(("arbitrary",)),
        name="moe_combine",
    )(p0, p0, p1, p1, ys, route, x, mod4)


def moe_ffn(h2, x, mod4, geo, gate_idx, r_w, w1, w3, w2, rows):
    D = geo.D
    E = w1.shape[0]
    F = w1.shape[2]
    tr = _pick(rows, (512, 256, 128, 64))
    route = router(h2, r_w, tr)
    e_flat = route[:, :TOP_K].astype(jnp.int32).reshape(-1)
    n_assign = rows * TOP_K
    tm = _pick(rows, (512, 256, 128, 64))
    onehot = (e_flat[:, None] == jnp.arange(E, dtype=jnp.int32)[None, :]).astype(jnp.int32)
    csum = jnp.cumsum(onehot, axis=0)
    rank = jnp.sum(csum * onehot, axis=1) - 1
    counts = csum[-1]
    padded = (counts + tm - 1) // tm * tm
    pad_end = jnp.cumsum(padded)
    pad_start = pad_end - padded
    pos = pad_start[e_flat] + rank
    P = n_assign + E * tm
    n_tiles = P // tm
    tok_flat = jnp.repeat(jnp.arange(rows, dtype=jnp.int32), TOP_K)
    buf_tok = jnp.zeros((P,), jnp.int32).at[pos].set(tok_flat)
    tile_e = jnp.minimum(jnp.searchsorted(pad_end, jnp.arange(n_tiles, dtype=jnp.int32) * tm, side="right"),
                         E - 1).astype(jnp.int32)
    n_used = (pad_end[-1] // tm).astype(jnp.int32).reshape(1)

    tg = _pick(rows, (256, 128, 64))
    xs = gather_rows(h2, buf_tok, tg, BF16)
    tn1 = _pick(F, (1024, 512, 256, 128))
    hb = grouped_mm(xs, [w1, w3], tile_e, n_used, tm=tm, tn=tn1, tk=D, epi="swiglu", out_dtype=BF16, name="moe_up")
    tn2 = _pick(D, (512, 256, 128))
    ys = grouped_mm(hb, [w2], tile_e, n_used, tm=tm, tn=tn2, tk=F, epi="none", out_dtype=F32, name="moe_down")
    pos2 = pos.reshape(rows, TOP_K)
    return moe_combine(ys, pos2[:, 0], pos2[:, 1], route, x, mod4, geo, gate_idx, rows, tg)


def _pack_w_in(w):
    D = w.shape[0]
    cols, at = [], 0
    for piece, off, width in sorted(_Z.values(), key=lambda v: v[1]):
        assert off == at
        lo, hi = _REF_OFF[piece], _REF_OFF[piece + 1]
        cols.append(w[:, lo:hi].astype(BF16))
        if width > hi - lo:
            cols.append(jnp.zeros((D, width - (hi - lo)), BF16))
        at += width
    assert at == Z_W
    return jnp.concatenate(cols, axis=1)


def kernel(x, c, ctx, c_ctx, ada_w, ada_b, norm1_g, norm2_g, w_in, mlstm_conv_w, mlstm_conv_b, mlstm_gate_b, mlstm_norm_g, conv_dw_w, conv_dw_b, conv_ln_g, conv_ln_b, gqa_q_norm_g, gqa_k_norm_g, diff_q_norm_g, diff_k_norm_g, diff_lambda, diff_subln_g, merge_gate_w, merge_gate_b, branch_w, out_w, ffn_w1, ffn_w3, ffn_w2, moe_router, moe_w1, moe_w3, moe_w2):
    B, S, D = x.shape
    L = ctx.shape[1]
    depth = ada_w.shape[0]
    geo = Geo(B, S, L, D)
    N, BS = geo.N, geo.BS
    assert S % GRID_W == 0 and S % M_CHUNK == 0 and L % M_CHUNK == 0 and BS % L == 0
    tb = _pick(math.gcd(S, L), (256, 128, 64))
    tm = _pick(math.gcd(S, B * L), (1024, 512, 256, 128))

    xt = jnp.concatenate([x.reshape(BS, D), ctx.reshape(B * L, D)], axis=0)
    cc = jnp.zeros((8, D), F32).at[:B].set(c).at[B].set(c_ctx)

    for l in range(depth):
        need_ctx = l < depth - 1
        rows = N if need_ctx else BS
        lam_init = 0.8 - 0.6 * math.exp(-0.3 * l)
        mod = adaln(cc, ada_w[l], ada_b[l])
        mod4 = mod[:B + 1].reshape(B + 1, 6, 1, D)

        h = norm_mod(xt, norm1_g[l], mod4, geo, 0, 1, N, BF16)
        z = dense_mm(h, [_pack_w_in(w_in[l])], rows=N, tm=tm, tn=_pick(Z_W, (1920, 640, 128)), tk=D, order="mn",
                     epi="none", out_dtype=F32, name="w_in")

        mq, mk = mlstm_qk(z, mlstm_conv_w[l], mlstm_conv_b[l], geo, tb)
        bc, cg = mlstm_gates(z, mlstm_gate_b[l], geo, tb)
        lanes = np.array([_gate_lane(d, hh) for d in range(2) for hh in range(M_HEADS)])
        cr = cg[:, lanes].reshape(N // M_CHUNK, M_CHUNK, 2 * M_HEADS).transpose(0, 2, 1)
        hf, hbw = mlstm_scan(mq, mk, z, bc, cg, cr, geo, tb)
        a_br = mlstm_out(hf, hbw, z, mlstm_norm_g[l], rows, tb)

        b_br = conformer(z, conv_dw_w[l], conv_dw_b[l], conv_ln_g[l], conv_ln_b[l], geo, tb)

        gq, gk, gv, xq, xk, xv = attn_prep(z, gqa_q_norm_g[l], gqa_k_norm_g[l], diff_q_norm_g[l], diff_k_norm_g[l],
                                           geo, tb)
        dkw = dict(lam=diff_lambda[l], sub_g=diff_subln_g[l], lam_init=lam_init)
        c_br = attention(gq, gk, gv, geo, n_kv=G_KV, d=G_HD, mode="gqa", ctx_queries=False)
        d_br = attention(xq, xk, xv, geo, n_kv=X_HEADS, d=X_HD, mode="diff", ctx_queries=False, **dkw)
        if need_ctx:
            c_ctx_o = attention(gq, gk, gv, geo, n_kv=G_KV, d=G_HD, mode="gqa", ctx_queries=True)
            d_ctx_o = attention(xq, xk, xv, geo, n_kv=X_HEADS, d=X_HD, mode="diff", ctx_queries=True, **dkw)
            c_br = jnp.concatenate([c_br, c_ctx_o], axis=0)
            d_br = jnp.concatenate([d_br, d_ctx_o], axis=0)

        tn = _pick(D, (512, 256, 128))
        y = merge(h, [a_br, b_br, c_br, d_br], merge_gate_w[l].astype(BF16), merge_gate_b[l].reshape(4, 1, D),
                  branch_w[l].astype(BF16), rows=rows, tm=tm, tn=tn)
        xt = dense_mm(y, [out_w[l].astype(BF16)], rows=rows, tm=tm, tn=tn, tk=D, order="mn", epi="resid",
                      out_dtype=F32, geo=geo, res=xt, mod4=mod4, gate_idx=2, name="out_proj")

        i = l // 2
        if l % 2 == 0:
            h2 = norm_mod(xt, norm2_g[l], mod4, geo, 3, 4, rows, BF16)
            F = ffn_w1.shape[2]
            tnf = _pick(F, (512, 256, 128))
            hb = dense_mm(h2, [ffn_w1[i].astype(BF16), ffn_w3[i].astype(BF16)], rows=rows, tm=tm, tn=tnf, tk=D,
                          order="mn", epi="swiglu", out_dtype=BF16, name="ffn_up")
            xt = dense_mm(hb, [ffn_w2[i].astype(BF16)], rows=rows, tm=tm, tn=tn, tk=F,
                          order="mn", epi="resid", out_dtype=F32, geo=geo, res=xt, mod4=mod4, gate_idx=5,
                          name="ffn_down")
        else:
            h2 = norm_mod(xt, norm2_g[l], mod4, geo, 3, 4, rows, F32)
            xt = moe_ffn(h2, xt, mod4, geo, 5, moe_router[i], moe_w1[i], moe_w3[i], moe_w2[i].astype(BF16), rows)
    return xt[:BS].reshape(B, S, D)
```

```python
import functools
import math

import numpy as np
import jax
import jax.numpy as jnp
from jax import lax
from jax.experimental import pallas as pl
from jax.experimental.pallas import tpu as pltpu

F32 = jnp.float32
BF16 = jnp.bfloat16

GRID_W = 64
M_HEADS, M_HD, M_CHUNK, M_CONV = 4, 128, 64, 3
M_W = M_HEADS * M_HD
C_W, C_K = 512, 31
G_HEADS, G_KV, G_HD = 4, 2, 128
X_HEADS, X_HD = 4, 64
X_VD = 2 * X_HD
ROPE_THETA = 10000.0
N_EXPERTS, TOP_K = 8, 2
EPS = 1e-6
LANE = 128

_REF_SPLITS = (2 * M_W, M_W, M_W, 4 * M_HEADS, 2 * C_W, G_HEADS * G_HD, G_KV * G_HD, G_KV * G_HD,
               X_HEADS * 2 * X_HD, X_HEADS * 2 * X_HD, X_HEADS * X_VD)
_REF_OFF = np.concatenate([[0], np.cumsum(_REF_SPLITS)]).tolist()
_Z = {
    "m_qk": (0, 0, 1024), "c_glu": (4, 1024, 1024), "m_v": (1, 2048, 512), "m_o": (2, 2560, 512),
    "g_q": (5, 3072, 512), "x_q": (8, 3584, 512), "x_k": (9, 4096, 512), "x_v": (10, 4608, 512),
    "g_k": (6, 5120, 256), "g_v": (7, 5376, 256), "m_g": (3, 5632, 128),
}
Z_W = 5760

_VMEM_LIMIT = 56 * 1024 * 1024


def _cp(sem, vmem=_VMEM_LIMIT):
    return pltpu.CompilerParams(dimension_semantics=sem, vmem_limit_bytes=vmem)


def _pick(n, cands):
    for c in cands:
        if n % c == 0:
            return c
    raise ValueError(f"no tile for {n} in {cands}")


def _sigmoid(x):
    return 1.0 / (1.0 + jnp.exp(-x))


def _silu(x):
    return x * _sigmoid(x)


def _adaln_kernel(c_ref, w_ref, b_ref, o_ref):
    s = _silu(c_ref[...]).astype(BF16)
    o_ref[...] = jnp.dot(s, w_ref[...].astype(BF16), preferred_element_type=F32) + b_ref[...]


def adaln(cc, w, b):
    R, D = cc.shape
    N = w.shape[1]
    tn = _pick(N, (512, 256, 128))
    return pl.pallas_call(
        _adaln_kernel,
        grid=(N // tn,),
        in_specs=[pl.BlockSpec((R, D), lambda n: (0, 0)),
                  pl.BlockSpec((D, tn), lambda n: (0, n)),
                  pl.BlockSpec((1, tn), lambda n: (0, n))],
        out_specs=pl.BlockSpec((R, tn), lambda n: (0, n)),
        out_shape=jax.ShapeDtypeStruct((R, N), F32),
        compiler_params=_cp(("parallel",)),
        name="adaln",
    )(cc, w, b.reshape(1, N))


class Geo:
    def __init__(self, B, S, L, D):
        self.B, self.S, self.L, self.D = B, S, L, D
        self.BS = B * S
        self.N = B * S + B * L

    def group(self, row0):
        return jnp.minimum(row0 // self.S, self.B)


def _norm_mod_kernel(x_ref, g_ref, sh_ref, sc_ref, o_ref):
    x = x_ref[...]
    y = x * lax.rsqrt(jnp.mean(x * x, axis=-1, keepdims=True) + EPS)
    y = y * g_ref[...]
    o_ref[...] = (y * (1.0 + sc_ref[...]) + sh_ref[...]).astype(o_ref.dtype)


def norm_mod(x, g, mod4, geo, i_shift, i_scale, rows, out_dtype):
    D = geo.D
    tm = _pick(math.gcd(geo.S, geo.B * geo.L), (512, 256, 128, 64))
    return pl.pallas_call(
        _norm_mod_kernel,
        grid=(rows // tm,),
        in_specs=[pl.BlockSpec((tm, D), lambda m: (m, 0)),
                  pl.BlockSpec((1, D), lambda m: (0, 0)),
                  pl.BlockSpec((None, None, 1, D), lambda m: (geo.group(m * tm), i_shift, 0, 0)),
                  pl.BlockSpec((None, None, 1, D), lambda m: (geo.group(m * tm), i_scale, 0, 0))],
        out_specs=pl.BlockSpec((tm, D), lambda m: (m, 0)),
        out_shape=jax.ShapeDtypeStruct((rows, D), out_dtype),
        compiler_params=_cp(("parallel",)),
        name="norm_mod",
    )(x, g.reshape(1, D), mod4, mod4)


def _bf16(x):
    return x if x.dtype == BF16 else x.astype(BF16)


def _mm_body(a_ref, w_refs, o_ref, acc_refs, nk, finalize):
    k = pl.program_id(2)
    a = _bf16(a_ref[...])
    parts = [jnp.dot(a, _bf16(w[...]), preferred_element_type=F32) for w in w_refs]
    if nk == 1:
        finalize(parts)
        return

    @pl.when(k == 0)
    def _():
        for acc, p in zip(acc_refs, parts):
            acc[...] = p

    @pl.when(k > 0)
    def _():
        for acc, p in zip(acc_refs, parts):
            acc[...] += p

    @pl.when(k == nk - 1)
    def _():
        finalize([acc[...] for acc in acc_refs])


def _dense_mm_kernel(*refs, n_w, nk, epi):
    refs = list(refs)
    a_ref = refs.pop(0)
    w_refs = [refs.pop(0) for _ in range(n_w)]
    extra = []
    if epi == "resid":
        extra = [refs.pop(0), refs.pop(0)]
    o_ref = refs.pop(0)

    def finalize(parts):
        if epi == "none":
            o_ref[...] = parts[0].astype(o_ref.dtype)
        elif epi == "swiglu":
            o_ref[...] = (_silu(parts[0]) * parts[1]).astype(o_ref.dtype)
        else:
            o_ref[...] = (extra[0][...] + extra[1][...] * parts[0]).astype(o_ref.dtype)

    _mm_body(a_ref, w_refs, o_ref, refs, nk, finalize)


def dense_mm(a, ws, *, rows, tm, tn, tk, order, epi, out_dtype, geo=None, res=None, mod4=None, gate_idx=None, name="mm"):
    K = a.shape[1]
    N = ws[0].shape[1]
    nm, nn, nk = rows // tm, N // tn, K // tk
    assert rows % tm == 0 and N % tn == 0 and K % tk == 0
    if order == "mn":
        grid = (nm, nn, nk)
        mi = lambda g0, g1: g0
        ni = lambda g0, g1: g1
    else:
        grid = (nn, nm, nk)
        mi = lambda g0, g1: g1
        ni = lambda g0, g1: g0
    in_specs = [pl.BlockSpec((tm, tk), lambda g0, g1, k: (mi(g0, g1), k))]
    in_specs += [pl.BlockSpec((tk, tn), lambda g0, g1, k: (k, ni(g0, g1))) for _ in ws]
    args = [a, *ws]
    if epi == "resid":
        in_specs.append(pl.BlockSpec((tm, tn), lambda g0, g1, k: (mi(g0, g1), ni(g0, g1))))
        in_specs.append(pl.BlockSpec((None, None, 1, tn),
                                     lambda g0, g1, k: (geo.group(mi(g0, g1) * tm), gate_idx, 0, ni(g0, g1))))
        args += [res, mod4]
    scratch = [pltpu.VMEM((tm, tn), F32) for _ in ws] if nk > 1 else []
    return pl.pallas_call(
        functools.partial(_dense_mm_kernel, n_w=len(ws), nk=nk, epi=epi),
        grid=grid,
        in_specs=in_specs,
        out_specs=pl.BlockSpec((tm, tn), lambda g0, g1, k: (mi(g0, g1), ni(g0, g1))),
        out_shape=jax.ShapeDtypeStruct((rows, N), out_dtype),
        scratch_shapes=scratch,
        compiler_params=_cp(("parallel", "parallel", "arbitrary")),
        name=name,
    )(*args)


def _grouped_mm_kernel(te_ref, nu_ref, *refs, n_w, nk, epi):
    refs = list(refs)
    a_ref = refs.pop(0)
    w_refs = [refs.pop(0) for _ in range(n_w)]
    o_ref = refs.pop(0)
    m = pl.program_id(1)
    k = pl.program_id(2)

    def finalize(parts):
        if epi == "swiglu":
            o_ref[...] = (_silu(parts[0]) * parts[1]).astype(o_ref.dtype)
        else:
            o_ref[...] = parts[0].astype(o_ref.dtype)

    @pl.when(m < nu_ref[0])
    def _():
        _mm_body(a_ref, w_refs, o_ref, refs, nk, finalize)

    @pl.when(jnp.logical_and(m >= nu_ref[0], k == nk - 1))
    def _():
        o_ref[...] = jnp.zeros_like(o_ref)


def grouped_mm(a, ws, tile_e, n_used, *, tm, tn, tk, epi, out_dtype, name="gmm"):
    P, K = a.shape
    N = ws[0].shape[2]
    nm, nn, nk = P // tm, N // tn, K // tk
    assert P % tm == 0 and N % tn == 0 and K % tk == 0

    def mc(m, nu):
        return jnp.minimum(m, nu[0] - 1)

    in_specs = [pl.BlockSpec((tm, tk), lambda n, m, k, te, nu: (mc(m, nu), k))]
    in_specs += [pl.BlockSpec((None, tk, tn), lambda n, m, k, te, nu: (te[mc(m, nu)], k, n)) for _ in ws]
    args = [a, *ws]
    scratch = [pltpu.VMEM((tm, tn), F32) for _ in ws] if nk > 1 else []
    return pl.pallas_call(
        functools.partial(_grouped_mm_kernel, n_w=len(ws), nk=nk, epi=epi),
        grid_spec=pltpu.PrefetchScalarGridSpec(
            num_scalar_prefetch=2,
            grid=(nn, nm, nk),
            in_specs=in_specs,
            out_specs=pl.BlockSpec((tm, tn), lambda n, m, k, te, nu: (m, n)),
            scratch_shapes=scratch),
        out_shape=jax.ShapeDtypeStruct((P, N), out_dtype),
        compiler_params=_cp(("parallel", "arbitrary", "arbitrary")),
        name=name,
    )(tile_e, n_used, *args)


def _merge_kernel(h_ref, wg_ref, bg_ref, b0_ref, b1_ref, b2_ref, b3_ref, wb_ref, o_ref):
    h = h_ref[...]
    acc = None
    for j, br in enumerate((b0_ref, b1_ref, b2_ref, b3_ref)):
        gate = _sigmoid(jnp.dot(h, wg_ref[j], preferred_element_type=F32) + bg_ref[j])
        term = gate * jnp.dot(br[...], wb_ref[j], preferred_element_type=F32)
        acc = term if acc is None else acc + term
    o_ref[...] = acc.astype(o_ref.dtype)


def merge(h, branches, wg, bg, wb, *, rows, tm, tn):
    D = h.shape[1]
    BW = branches[0].shape[1]
    nb = len(branches)
    br_spec = pl.BlockSpec((tm, BW), lambda m, n: (m, 0))
    return pl.pallas_call(
        _merge_kernel,
        grid=(rows // tm, D // tn),
        in_specs=[pl.BlockSpec((tm, D), lambda m, n: (m, 0)),
                  pl.BlockSpec((nb, D, tn), lambda m, n: (0, 0, n)),
                  pl.BlockSpec((nb, 1, tn), lambda m, n: (0, 0, n)),
                  br_spec, br_spec, br_spec, br_spec,
                  pl.BlockSpec((nb, BW, tn), lambda m, n: (0, 0, n))],
        out_specs=pl.BlockSpec((tm, tn), lambda m, n: (m, n)),
        out_shape=jax.ShapeDtypeStruct((rows, D), BF16),
        compiler_params=_cp(("parallel", "parallel")),
        name="merge",
    )(h, wg, bg, *branches, wb)


def _seq_flags(geo, tb):
    row0 = pl.program_id(0) * tb
    lat = row0 < geo.BS
    rel = row0 - geo.BS
    start = jnp.where(lat, row0 % geo.S == 0, rel % geo.L == 0)
    end = jnp.where(lat, (row0 + tb) % geo.S == 0, (rel + tb) % geo.L == 0)
    return start, end


def _halo_specs(geo, tb, hb, width, col_block):
    r = tb // hb
    last = geo.N // hb - 1
    return [pl.BlockSpec((tb, width), lambda i: (i, col_block)),
            pl.BlockSpec((hb, width), lambda i: (jnp.maximum(i * r - 1, 0), col_block)),
            pl.BlockSpec((hb, width), lambda i: (jnp.minimum((i + 1) * r, last), col_block))]


def _conv_rows(u_ref, w_ref, base, r0, rows, K):
    acc = None
    for j in range(K):
        t = w_ref[j:j + 1, :] * u_ref[pl.ds(base + r0 + j, rows), :]
        acc = t if acc is None else acc + t
    return acc


_C_HB = 16


def _conformer_kernel(main_ref, prev_ref, next_ref, w_ref, b_ref, lg_ref, lb_ref, o_ref, u_ref, *, geo, tb):
    start, end = _seq_flags(geo, tb)

    def glu(blk):
        return blk[:, :C_W] * _sigmoid(blk[:, C_W:])

    u_ref[pl.ds(_C_HB, tb), :] = glu(main_ref[...])
    u_ref[pl.ds(0, _C_HB), :] = jnp.where(start, 0.0, glu(prev_ref[...]))
    u_ref[pl.ds(_C_HB + tb, _C_HB), :] = jnp.where(end, 0.0, glu(next_ref[...]))
    rc = 32
    for r0 in range(0, tb, rc):
        u = _conv_rows(u_ref, w_ref, _C_HB - C_K // 2, r0, rc, C_K) + b_ref[...]
        mu = jnp.mean(u, axis=-1, keepdims=True)
        xc = u - mu
        y = xc * lax.rsqrt(jnp.mean(xc * xc, axis=-1, keepdims=True) + EPS)
        y = y * lg_ref[...] + lb_ref[...]
        o_ref[pl.ds(r0, rc), :] = _silu(y).astype(o_ref.dtype)


def conformer(z, w, b, lg, lb, geo, tb):
    cb = _Z["c_glu"][1] // (2 * C_W)
    return pl.pallas_call(
        functools.partial(_conformer_kernel, geo=geo, tb=tb),
        grid=(geo.N // tb,),
        in_specs=_halo_specs(geo, tb, _C_HB, 2 * C_W, cb) + [
            pl.BlockSpec((C_K, C_W), lambda i: (0, 0)),
            pl.BlockSpec((1, C_W), lambda i: (0, 0)),
            pl.BlockSpec((1, C_W), lambda i: (0, 0)),
            pl.BlockSpec((1, C_W), lambda i: (0, 0))],
        out_specs=pl.BlockSpec((tb, C_W), lambda i: (i, 0)),
        out_shape=jax.ShapeDtypeStruct((geo.N, C_W), BF16),
        scratch_shapes=[pltpu.VMEM((tb + 2 * _C_HB, C_W), F32)],
        compiler_params=_cp(("parallel",)),
        name="conformer",
    )(z, z, z, w, b.reshape(1, C_W), lg.reshape(1, C_W), lb.reshape(1, C_W))


_M_HB = 8


def _mlstm_qk_kernel(main_ref, prev_ref, next_ref, w_ref, b_ref, q_ref, k_ref, u_ref, *, geo, tb):
    start, end = _seq_flags(geo, tb)
    u_ref[pl.ds(_M_HB, tb), :] = main_ref[...]
    u_ref[pl.ds(0, _M_HB), :] = jnp.where(start, 0.0, prev_ref[...])
    u_ref[pl.ds(_M_HB + tb, _M_HB), :] = jnp.where(end, 0.0, next_ref[...])
    rc = 32
    for r0 in range(0, tb, rc):
        y = _silu(_conv_rows(u_ref, w_ref, _M_HB - M_CONV // 2, r0, rc, M_CONV) + b_ref[...])
        q_ref[pl.ds(r0, rc), :] = y[:, :M_W].astype(q_ref.dtype)
        k_ref[pl.ds(r0, rc), :] = (y[:, M_W:] * (M_HD ** -0.5)).astype(k_ref.dtype)


def mlstm_qk(z, w, b, geo, tb):
    cb = _Z["m_qk"][1] // (2 * M_W)
    return pl.pallas_call(
        functools.partial(_mlstm_qk_kernel, geo=geo, tb=tb),
        grid=(geo.N // tb,),
        in_specs=_halo_specs(geo, tb, _M_HB, 2 * M_W, cb) + [
            pl.BlockSpec((M_CONV, 2 * M_W), lambda i: (0, 0)),
            pl.BlockSpec((1, 2 * M_W), lambda i: (0, 0))],
        out_specs=[pl.BlockSpec((tb, M_W), lambda i: (i, 0)), pl.BlockSpec((tb, M_W), lambda i: (i, 0))],
        out_shape=[jax.ShapeDtypeStruct((geo.N, M_W), BF16), jax.ShapeDtypeStruct((geo.N, M_W), BF16)],
        scratch_shapes=[pltpu.VMEM((tb + 2 * _M_HB, 2 * M_W), F32)],
        compiler_params=_cp(("parallel",)),
        name="mlstm_qk",
    )(z, z, z, w, b.reshape(1, 2 * M_W))


def _split3(x):
    hi = x.astype(BF16)
    r1 = x - hi.astype(F32)
    mid = r1.astype(BF16)
    lo = (r1 - mid.astype(F32)).astype(BF16)
    return hi, mid, lo


def _log_sigmoid(x):
    return jnp.minimum(x, 0.0) - jnp.log(1.0 + jnp.exp(-jnp.abs(x)))


def _mlstm_gate_kernel(g_ref, b_ref, bc_ref, cc_ref, *, tb):
    raw = g_ref[...] + b_ref[...]
    lf = _log_sigmoid(raw)
    t = lax.broadcasted_iota(jnp.int32, (tb, tb), 0)
    s = lax.broadcasted_iota(jnp.int32, (tb, tb), 1)
    same = (t // M_CHUNK) == (s // M_CHUNK)
    lower = jnp.where(jnp.logical_and(same, s <= t), 1.0, 0.0).astype(BF16)
    upper = jnp.where(jnp.logical_and(same, s >= t), 1.0, 0.0).astype(BF16)
    hi, mid, lo = _split3(lf)

    def csum(tri):
        d = lambda p: jnp.dot(tri, p, preferred_element_type=F32)
        return (d(lo) + d(mid)) + d(hi)

    lane = lax.broadcasted_iota(jnp.int32, (tb, LANE), 1)
    b_all = jnp.where(lane < 2 * M_HEADS, csum(lower), csum(upper))
    i_sh = pltpu.roll(raw, M_HEADS, axis=1)
    bc_ref[...] = b_all
    cc_ref[...] = i_sh - b_all


def mlstm_gates(z, gate_b, geo, tb):
    cb = _Z["m_g"][1] // LANE
    gb = jnp.zeros((1, LANE), F32).at[0, :4 * M_HEADS].set(gate_b)
    spec = pl.BlockSpec((tb, LANE), lambda i: (i, 0))
    return pl.pallas_call(
        functools.partial(_mlstm_gate_kernel, tb=tb),
        grid=(geo.N // tb,),
        in_specs=[pl.BlockSpec((tb, LANE), lambda i: (i, cb)), pl.BlockSpec((1, LANE), lambda i: (0, 0))],
        out_specs=[spec, spec],
        out_shape=[jax.ShapeDtypeStruct((geo.N, LANE), F32)] * 2,
        compiler_params=_cp(("parallel",)),
        name="mlstm_gates",
    )(z, gb)


def _gate_lane(d, h):
    return 4 + 8 * d + h


def _mlstm_scan_kernel(*refs, tb):
    (qf, kf, vf, bcf, ccf, crf, qb, kb, vb, bcb, ccb, crb, hf_ref, hb_ref, ct_ref, m_ref) = refs
    i = pl.program_id(1)

    @pl.when(i == 0)
    def _():
        ct_ref[...] = jnp.zeros_like(ct_ref)
        m_ref[...] = jnp.zeros_like(m_ref)

    nc = tb // M_CHUNK
    t_i = lax.broadcasted_iota(jnp.int32, (M_CHUNK, M_CHUNK), 0)
    s_i = lax.broadcasted_iota(jnp.int32, (M_CHUNK, M_CHUNK), 1)
    e0 = jnp.where(lax.broadcasted_iota(jnp.int32, (M_CHUNK, M_HD), 1) == 0, 1.0, 0.0)
    dirs = ((0, qf, kf, vf, bcf, ccf, crf, hf_ref, s_i <= t_i), (1, qb, kb, vb, bcb, ccb, crb, hb_ref, s_i >= t_i))
    for d, q_ref, k_ref, v_ref, bc_ref, cc_ref, cr_ref, h_ref, mask in dirs:
        chunks = range(nc) if d == 0 else range(nc - 1, -1, -1)
        last = M_CHUNK - 1 if d == 0 else 0
        for h in range(M_HEADS):
            ln = _gate_lane(d, h)
            sid = d * M_HEADS + h
            m_prev = m_ref[sid, 0:1, 0:1]
            ct = ct_ref[sid]
            for c in chunks:
                rows = pl.ds(c * M_CHUNK, M_CHUNK)
                lanes = slice(h * M_HD, (h + 1) * M_HD)
                qc = q_ref[rows, lanes]
                kc = k_ref[rows, lanes]
                vc = v_ref[rows, lanes]
                col_r = cr_ref[c, sid:sid + 1, :]
                col_c = cc_ref[rows, ln:ln + 1]
                b_c = bc_ref[rows, ln:ln + 1]
                cm = jnp.max(jnp.where(mask, col_r, -jnp.inf), axis=-1, keepdims=True)
                mt = jnp.maximum(m_prev, cm)
                s = lax.dot_general(qc, kc, (((1,), (1,)), ((), ())), preferred_element_type=F32)
                w = jnp.where(mask, jnp.exp(col_r - mt), 0.0) * s
                v_aug = jnp.concatenate([vc, e0], axis=1)
                tot = jnp.exp(m_prev - mt) * jnp.dot(qc, ct.astype(BF16), preferred_element_type=F32)
                tot = tot + jnp.dot(w.astype(BF16), v_aug.astype(BF16), preferred_element_type=F32)
                den = jnp.maximum(jnp.abs(tot[:, M_HD:M_HD + 1]), jnp.exp(-b_c - mt))
                h_ref[rows, lanes] = tot[:, :M_HD] / den
                m_end = mt[last:last + 1, :]
                wg = jnp.exp(col_c - m_end)
                upd = lax.dot_general(kc, (wg * v_aug).astype(BF16), (((0,), (0,)), ((), ())),
                                      preferred_element_type=F32)
                ct = jnp.exp(m_prev - m_end) * ct + upd
                m_prev = b_c[last:last + 1, :] + m_end
            ct_ref[sid] = ct
            m_ref[sid] = jnp.broadcast_to(m_prev, m_ref.shape[1:])


def mlstm_scan(q, k, z, bc, cc, cr, geo, tb):
    B, S, L = geo.B, geo.S, geo.L
    nctx, nlat = L // tb, S // tb
    nblk = nctx + nlat
    vcb = _Z["m_v"][1] // M_W

    def blk(b, i, d):
        ctx_i = i if d == 0 else nctx - 1 - i
        lat_i = i - nctx if d == 0 else nlat - 1 - (i - nctx)
        return jnp.where(i < nctx, (B * S) // tb + b * nctx + ctx_i, b * nlat + lat_i)

    def specs(d):
        return [pl.BlockSpec((tb, M_W), lambda b, i: (blk(b, i, d), 0)),
                pl.BlockSpec((tb, M_W), lambda b, i: (blk(b, i, d), 0)),
                pl.BlockSpec((tb, M_W), lambda b, i: (blk(b, i, d), vcb)),
                pl.BlockSpec((tb, LANE), lambda b, i: (blk(b, i, d), 0)),
                pl.BlockSpec((tb, LANE), lambda b, i: (blk(b, i, d), 0)),
                pl.BlockSpec((tb // M_CHUNK, 2 * M_HEADS, M_CHUNK), lambda b, i: (blk(b, i, d), 0, 0))]

    out_f = pl.BlockSpec((tb, M_W), lambda b, i: (blk(b, i, 0), 0))
    out_b = pl.BlockSpec((tb, M_W), lambda b, i: (blk(b, i, 1), 0))
    return pl.pallas_call(
        functools.partial(_mlstm_scan_kernel, tb=tb),
        grid=(B, nblk),
        in_specs=specs(0) + specs(1),
        out_specs=[out_f, out_b],
        out_shape=[jax.ShapeDtypeStruct((geo.N, M_W), F32)] * 2,
        scratch_shapes=[pltpu.VMEM((2 * M_HEADS, M_HD, 2 * M_HD), F32), pltpu.VMEM((2 * M_HEADS, 8, LANE), F32)],
        compiler_params=_cp(("parallel", "arbitrary")),
        name="mlstm_scan",
    )(q, k, z, bc, cc, cr, q, k, z, bc, cc, cr)


def _mlstm_out_kernel(hf_ref, hb_ref, o_ref, g_ref, a_ref):
    hsum = hf_ref[...] + hb_ref[...]
    gate = _sigmoid(o_ref[...])
    for h in range(M_HEADS):
        lanes = slice(h * M_HD, (h + 1) * M_HD)
        x = hsum[:, lanes]
        y = x * lax.rsqrt(jnp.mean(x * x, axis=-1, keepdims=True) + EPS) * g_ref[:, lanes]
        a_ref[:, lanes] = (y * gate[:, lanes]).astype(a_ref.dtype)


def mlstm_out(hf, hb, z, g, rows, tb):
    ocb = _Z["m_o"][1] // M_W
    spec = pl.BlockSpec((tb, M_W), lambda i: (i, 0))
    return pl.pallas_call(
        _mlstm_out_kernel,
        grid=(rows // tb,),
        in_specs=[spec, spec, pl.BlockSpec((tb, M_W), lambda i: (i, ocb)), pl.BlockSpec((1, M_W), lambda i: (0, 0))],
        out_specs=spec,
        out_shape=jax.ShapeDtypeStruct((rows, M_W), BF16),
        compiler_params=_cp(("parallel",)),
        name="mlstm_out",
    )(hf, hb, z, g.reshape(1, M_W))


def _rope_tables(S, d):
    r = d // 4
    t = np.arange(S)
    freqs = ROPE_THETA ** (-np.arange(r, dtype=np.float32) / r)
    rows = (t // GRID_W).astype(np.float32)[:, None] * freqs
    cols = (t % GRID_W).astype(np.float32)[:, None] * freqs
    rows, cols = jnp.asarray(rows, F32), jnp.asarray(cols, F32)
    cos = jnp.concatenate([jnp.cos(rows), jnp.cos(rows), jnp.cos(cols), jnp.cos(cols)], axis=1)
    sin = jnp.concatenate([-jnp.sin(rows), jnp.sin(rows), -jnp.sin(cols), jnp.sin(cols)], axis=1)
    rep = LANE // d
    return jnp.tile(cos, (1, rep)), jnp.tile(sin, (1, rep))


def _norm_rope(x, g, cos, sin, is_lat, d):
    r = d // 4
    if d == LANE:
        ms = jnp.mean(x * x, axis=-1, keepdims=True)
    else:
        parts = [jnp.broadcast_to(jnp.mean(x[:, o:o + d] * x[:, o:o + d], axis=-1, keepdims=True), (x.shape[0], d))
                 for o in range(0, LANE, d)]
        ms = jnp.concatenate(parts, axis=1)
    y = x * lax.rsqrt(ms + EPS) * g
    lane = lax.broadcasted_iota(jnp.int32, y.shape, 1)
    first = (lane // r) % 2 == 0
    partner = jnp.where(first, pltpu.roll(y, LANE - r, axis=1), pltpu.roll(y, r, axis=1))
    return jnp.where(is_lat, y * cos + partner * sin, y)


def _attn_prep_kernel(gq_ref, gk_ref, gv_ref, xq_ref, xk_ref, xv_ref, gqg_ref, gkg_ref, xqg_ref, xkg_ref,
                      cg_ref, sg_ref, cx_ref, sx_ref, oq_ref, ok_ref, ov_ref, oxq_ref, oxk_ref, oxv_ref, *, geo, tb):
    is_lat = pl.program_id(0) * tb < geo.BS
    cg, sg, cx, sx = cg_ref[...], sg_ref[...], cx_ref[...], sx_ref[...]
    for h in range(G_HEADS):
        ls = slice(h * LANE, (h + 1) * LANE)
        oq_ref[:, ls] = _norm_rope(gq_ref[:, ls], gqg_ref[...], cg, sg, is_lat, G_HD).astype(BF16)
    for h in range(G_KV):
        ls = slice(h * LANE, (h + 1) * LANE)
        ok_ref[:, ls] = _norm_rope(gk_ref[:, ls], gkg_ref[...], cg, sg, is_lat, G_HD).astype(BF16)
    ov_ref[...] = gv_ref[...].astype(BF16)
    lane = lax.broadcasted_iota(jnp.int32, (tb, LANE), 1)
    for h in range(X_HEADS):
        ls = slice(h * LANE, (h + 1) * LANE)
        q = _norm_rope(xq_ref[:, ls], xqg_ref[...], cx, sx, is_lat, X_HD)
        oxq_ref[:, 2 * h * LANE:(2 * h + 1) * LANE] = jnp.where(lane < X_HD, q, 0.0).astype(BF16)
        oxq_ref[:, (2 * h + 1) * LANE:(2 * h + 2) * LANE] = jnp.where(lane >= X_HD, q, 0.0).astype(BF16)
        oxk_ref[:, ls] = _norm_rope(xk_ref[:, ls], xkg_ref[...], cx, sx, is_lat, X_HD).astype(BF16)
    oxv_ref[...] = xv_ref[...].astype(BF16)


def attn_prep(z, gq_g, gk_g, xq_g, xk_g, geo, tb):
    N = geo.N
    cg, sg = _rope_tables(geo.S, G_HD)
    cx, sx = _rope_tables(geo.S, X_HD)
    nlat_t = geo.S // tb

    def zs(name, width):
        cb = _Z[name][1] // width
        return pl.BlockSpec((tb, width), lambda i: (i, cb))

    tab = pl.BlockSpec((tb, LANE), lambda i: (i % nlat_t, 0))
    vec = pl.BlockSpec((1, LANE), lambda i: (0, 0))
    row = lambda w: pl.BlockSpec((tb, w), lambda i: (i, 0))
    rep = LANE // X_HD
    return pl.pallas_call(
        functools.partial(_attn_prep_kernel, geo=geo, tb=tb),
        grid=(N // tb,),
        in_specs=[zs("g_q", 512), zs("g_k", 256), zs("g_v", 256), zs("x_q", 512), zs("x_k", 512), zs("x_v", 512),
                  vec, vec, vec, vec, tab, tab, tab, tab],
        out_specs=[row(512), row(256), row(256), row(1024), row(512), row(512)],
        out_shape=[jax.ShapeDtypeStruct((N, w), BF16) for w in (512, 256, 256, 1024, 512, 512)],
        compiler_params=_cp(("parallel",)),
        name="attn_prep",
    )(z, z, z, z, z, z, gq_g.reshape(1, LANE), gk_g.reshape(1, LANE),
      jnp.tile(xq_g, rep).reshape(1, LANE), jnp.tile(xk_g, rep).reshape(1, LANE), cg, sg, cx, sx)


def _flash_kernel(*refs, scale, tk, n_lat, mode, lam_init):
    refs = list(refs)
    q_ref = refs.pop(0)
    if n_lat:
        kl_ref, vl_ref = refs.pop(0), refs.pop(0)
    kc_ref, vc_ref = refs.pop(0), refs.pop(0)
    if mode == "diff":
        lam_ref, sg_ref = refs.pop(0), refs.pop(0)
    o_ref = refs.pop(0)
    tq = q_ref.shape[0]
    q = jnp.concatenate([q_ref[:, :LANE], q_ref[:, LANE:]], axis=0)
    c_exp = scale * math.log2(math.e)

    def ones_col(n):
        return jnp.where(lax.broadcasted_iota(jnp.int32, (n, LANE), 1) == 0, 1.0, 0.0).astype(BF16)

    def chunk(k, v, e0, carry):
        m, acc = carry
        s = lax.dot_general(q, k, (((1,), (1,)), ((), ())), preferred_element_type=F32)
        m_new = jnp.maximum(m, jnp.max(s, axis=-1, keepdims=True))
        p = jnp.exp2((s - m_new) * c_exp)
        alpha = jnp.exp2((m - m_new) * c_exp)
        v_aug = jnp.concatenate([v, e0], axis=1)
        acc = alpha * acc + jnp.dot(p.astype(BF16), v_aug, preferred_element_type=F32)
        return m_new, acc

    carry = (jnp.full((2 * tq, 1), -jnp.inf, F32), jnp.zeros((2 * tq, 2 * LANE), F32))
    if n_lat:
        e_lat = ones_col(tk)

        def body(j, carry):
            rows = pl.ds(pl.multiple_of(j * tk, tk), tk)
            return chunk(kl_ref[rows, :], vl_ref[rows, :], e_lat, carry)

        carry = lax.fori_loop(0, n_lat, body, carry, unroll=8 if n_lat % 8 == 0 else 1)
    _, acc = chunk(kc_ref[...], vc_ref[...], ones_col(kc_ref.shape[0]), carry)
    o = acc[:, :LANE] / acc[:, LANE:LANE + 1]
    outs = [o[:tq], o[tq:]]
    if mode == "gqa":
        o_ref[:, :LANE] = outs[0].astype(o_ref.dtype)
        o_ref[:, LANE:] = outs[1].astype(o_ref.dtype)
    else:
        dl = lam_ref[...]
        lam = (jnp.exp(jnp.sum(dl[0:1] * dl[1:2], axis=-1, keepdims=True))
               - jnp.exp(jnp.sum(dl[2:3] * dl[3:4], axis=-1, keepdims=True)) + lam_init)
        x = outs[0] - lam * outs[1]
        y = x * lax.rsqrt(jnp.mean(x * x, axis=-1, keepdims=True) + EPS) * sg_ref[...]
        o_ref[...] = (y * (1.0 - lam_init)).astype(o_ref.dtype)


def attention(q, k, v, geo, *, n_kv, d, mode, ctx_queries, lam=None, sub_g=None, lam_init=0.0):
    B, S, L = geo.B, geo.S, geo.L
    ctx0 = (B * S) // L
    if ctx_queries:
        tq, nq, n_lat, tk = L, 1, 0, L
        qrow = lambda b, h, i: ctx0 + b
        orow = lambda b, h, i: b
    else:
        tq = _pick(S, (512, 256, 128, 64))
        nq = S // tq
        tk = _pick(S, (1024, 512, 256, 128, 64))
        n_lat = S // tk
        qrow = lambda b, h, i: b * nq + i
        orow = qrow
    n_out = B * tq * nq
    in_specs = [pl.BlockSpec((tq, 2 * LANE), lambda b, h, i: (qrow(b, h, i), h))]
    args = [q]
    if n_lat:
        in_specs += [pl.BlockSpec((S, LANE), lambda b, h, i: (b, h)), pl.BlockSpec((S, LANE), lambda b, h, i: (b, h))]
        args += [k, v]
    in_specs += [pl.BlockSpec((L, LANE), lambda b, h, i: (ctx0 + b, h)),
                 pl.BlockSpec((L, LANE), lambda b, h, i: (ctx0 + b, h))]
    args += [k, v]
    if mode == "diff":
        in_specs += [pl.BlockSpec((4, X_HD), lambda b, h, i: (0, 0)), pl.BlockSpec((1, LANE), lambda b, h, i: (0, 0))]
        args += [lam, sub_g.reshape(1, LANE)]
        ow = LANE
    else:
        ow = 2 * LANE
    return pl.pallas_call(
        functools.partial(_flash_kernel, scale=d ** -0.5, tk=tk, n_lat=n_lat, mode=mode, lam_init=lam_init),
        grid=(B, n_kv, nq),
        in_specs=in_specs,
        out_specs=pl.BlockSpec((tq, ow), lambda b, h, i: (orow(b, h, i), h)),
        out_shape=jax.ShapeDtypeStruct((n_out, n_kv * ow), BF16),
        compiler_params=_cp(("parallel", "parallel", "arbitrary")),
        name="attn_" + mode + ("_ctx" if ctx_queries else ""),
    )(*args)


def _router_kernel(t_ref, r_ref, o_ref):
    t = t_ref[...]
    th, tm_, _ = _split3(t)
    rh, rm, _ = _split3(r_ref[...])
    d = lambda a, b: jnp.dot(a, b, preferred_element_type=F32)
    logits = (d(tm_, rh) + d(th, rm)) + d(th, rh)
    lane = lax.broadcasted_iota(jnp.int32, logits.shape, 1).astype(F32)
    logits = jnp.where(lane < N_EXPERTS, logits, -jnp.inf)
    v1 = jnp.max(logits, axis=-1, keepdims=True)
    i1 = jnp.min(jnp.where(logits == v1, lane, float(LANE)), axis=-1, keepdims=True)
    rest = jnp.where(lane == i1, -jnp.inf, logits)
    v2 = jnp.max(rest, axis=-1, keepdims=True)
    i2 = jnp.min(jnp.where(rest == v2, lane, float(LANE)), axis=-1, keepdims=True)
    e2 = jnp.exp(v2 - v1)
    den = 1.0 + e2
    out = jnp.where(lane == 0, i1, 0.0)
    out = jnp.where(lane == 1, i2, out)
    out = jnp.where(lane == 2, 1.0 / den, out)
    out = jnp.where(lane == 3, e2 / den, out)
    o_ref[...] = out


def router(t, r, tm):
    N, D = t.shape
    rp = jnp.zeros((D, LANE), F32).at[:, :N_EXPERTS].set(r)
    return pl.pallas_call(
        _router_kernel,
        grid=(N // tm,),
        in_specs=[pl.BlockSpec((tm, D), lambda m: (m, 0)), pl.BlockSpec((D, LANE), lambda m: (0, 0))],
        out_specs=pl.BlockSpec((tm, LANE), lambda m: (m, 0)),
        out_shape=jax.ShapeDtypeStruct((N, LANE), F32),
        compiler_params=_cp(("parallel",)),
        name="router",
    )(t, rp)


def _row_copy(src_hbm, dst_vmem, sem, src_row, dst_row):
    return pltpu.make_async_copy(src_hbm.at[pl.ds(src_row, 1), :], dst_vmem.at[pl.ds(dst_row, 1), :], sem)


def _gather_issue(src_hbm, idx_smem, dst_vmem, sem, n):
    def issue(r, c):
        _row_copy(src_hbm, dst_vmem, sem, idx_smem[0, 0, r], r).start()
        return c

    lax.fori_loop(0, n, issue, 0, unroll=8)


def _gather_drain(src_hbm, dst_vmem, sem, n):
    def drain(r, c):
        _row_copy(src_hbm, dst_vmem, sem, 0, r).wait()
        return c

    lax.fori_loop(0, n, drain, 0, unroll=8)


def _load_idx(idx_vmem, idx_smem, sem):
    cp = pltpu.make_async_copy(idx_vmem, idx_smem, sem)
    cp.start()
    cp.wait()


def _idx_specs(tg, nsteps):
    cur = pl.BlockSpec((1, 1, tg), lambda i: (i, 0, 0))
    nxt = pl.BlockSpec((1, 1, tg), lambda i: (jnp.minimum(i + 1, nsteps - 1), 0, 0))
    return cur, nxt


def _gather_kernel(idx_ref, idx_next_ref, src_ref, o_ref, idx_smem, buf, sem_i, sem_r, *, tg):
    i = pl.program_id(0)
    slot = i % 2

    def fetch(idx_vmem, s):
        _load_idx(idx_vmem, idx_smem.at[s], sem_i)
        _gather_issue(src_ref, idx_smem.at[s], buf.at[s], sem_r.at[s], tg)

    @pl.when(i == 0)
    def _():
        fetch(idx_ref, 0)

    @pl.when(i + 1 < pl.num_programs(0))
    def _():
        fetch(idx_next_ref, 1 - slot)

    _gather_drain(src_ref, buf.at[slot], sem_r.at[slot], tg)
    o_ref[...] = buf[slot].astype(o_ref.dtype)


def gather_rows(src, idx, tg, out_dtype):
    P = idx.shape[0]
    D = src.shape[1]
    nsteps = P // tg
    idx3 = idx.reshape(nsteps, 1, tg)
    return pl.pallas_call(
        functools.partial(_gather_kernel, tg=tg),
        grid=(nsteps,),
        in_specs=[*_idx_specs(tg, nsteps), pl.BlockSpec(memory_space=pl.ANY)],
        out_specs=pl.BlockSpec((tg, D), lambda i: (i, 0)),
        out_shape=jax.ShapeDtypeStruct((P, D), out_dtype),
        scratch_shapes=[pltpu.SMEM((2, 1, 1, tg), jnp.int32), pltpu.VMEM((2, tg, D), src.dtype),
                        pltpu.SemaphoreType.DMA(()), pltpu.SemaphoreType.DMA((2,))],
        compiler_params=_cp(("arbitrary",)),
        name="moe_gather",
    )(idx3, idx3, src)


def _combine_kernel(p0_ref, p0_next_ref, p1_ref, p1_next_ref, ys_ref, route_ref, x_ref, gate_ref, o_ref,
                    s0, s1, b0, b1, sem_i, sem_r0, sem_r1, *, tg):
    i = pl.program_id(0)
    slot = i % 2

    def fetch(p0_vmem, p1_vmem, s):
        _load_idx(p0_vmem, s0.at[s], sem_i)
        _load_idx(p1_vmem, s1.at[s], sem_i)
        _gather_issue(ys_ref, s0.at[s], b0.at[s], sem_r0.at[s], tg)
        _gather_issue(ys_ref, s1.at[s], b1.at[s], sem_r1.at[s], tg)

    @pl.when(i == 0)
    def _():
        fetch(p0_ref, p1_ref, 0)

    @pl.when(i + 1 < pl.num_programs(0))
    def _():
        fetch(p0_next_ref, p1_next_ref, 1 - slot)

    _gather_drain(ys_ref, b0.at[slot], sem_r0.at[slot], tg)
    _gather_drain(ys_ref, b1.at[slot], sem_r1.at[slot], tg)
    w0 = route_ref[:, TOP_K:TOP_K + 1]
    w1 = route_ref[:, TOP_K + 1:TOP_K + 2]
    o_ref[...] = x_ref[...] + gate_ref[...] * (b0[slot] * w0 + b1[slot] * w1)


def moe_combine(ys, pos0, pos1, route, x, mod4, geo, gate_idx, rows, tg):
    D = geo.D
    nsteps = rows // tg
    cur, nxt = _idx_specs(tg, nsteps)
    p0, p1 = pos0.reshape(nsteps, 1, tg), pos1.reshape(nsteps, 1, tg)
    return pl.pallas_call(
        functools.partial(_combine_kernel, tg=tg),
        grid=(nsteps,),
        in_specs=[cur, nxt, cur, nxt, pl.BlockSpec(memory_space=pl.ANY),
                  pl.BlockSpec((tg, LANE), lambda i: (i, 0)),
                  pl.BlockSpec((tg, D), lambda i: (i, 0)),
                  pl.BlockSpec((None, None, 1, D), lambda i: (geo.group(i * tg), gate_idx, 0, 0))],
        out_specs=pl.BlockSpec((tg, D), lambda i: (i, 0)),
        out_shape=jax.ShapeDtypeStruct((rows, D), F32),
        scratch_shapes=[pltpu.SMEM((2, 1, 1, tg), jnp.int32), pltpu.SMEM((2, 1, 1, tg), jnp.int32),
                        pltpu.VMEM((2, tg, D), F32), pltpu.VMEM((2, tg, D), F32),
                        pltpu.SemaphoreType.DMA(()), pltpu.SemaphoreType.DMA((2,)), pltpu.SemaphoreType.DMA((2,))],
        compiler_params=_cp(("arbitrary",)),
        name="moe_combine",
    )(p0, p0, p1, p1, ys, route, x, mod4)


def moe_ffn(h2, x, mod4, geo, gate_idx, r_w, w1, w3, w2, rows):
    D = geo.D
    E = w1.shape[0]
    F = w1.shape[2]
    tr = _pick(rows, (512, 256, 128, 64))
    route = router(h2, r_w, tr)
    e_flat = route[:, :TOP_K].astype(jnp.int32).reshape(-1)
    n_assign = rows * TOP_K
    tm = _pick(rows, (512, 256, 128, 64))
    onehot = (e_flat[:, None] == jnp.arange(E, dtype=jnp.int32)[None, :]).astype(jnp.int32)
    csum = jnp.cumsum(onehot, axis=0)
    rank = jnp.sum(csum * onehot, axis=1) - 1
    counts = csum[-1]
    padded = (counts + tm - 1) // tm * tm
    pad_end = jnp.cumsum(padded)
    pad_start = pad_end - padded
    pos = pad_start[e_flat] + rank
    P = n_assign + E * tm
    n_tiles = P // tm
    tok_flat = jnp.repeat(jnp.arange(rows, dtype=jnp.int32), TOP_K)
    buf_tok = jnp.zeros((P,), jnp.int32).at[pos].set(tok_flat)
    tile_e = jnp.minimum(jnp.searchsorted(pad_end, jnp.arange(n_tiles, dtype=jnp.int32) * tm, side="right"),
                         E - 1).astype(jnp.int32)
    n_used = (pad_end[-1] // tm).astype(jnp.int32).reshape(1)

    tg = _pick(rows, (256, 128, 64))
    xs = gather_rows(h2, buf_tok, tg, BF16)
    tn1 = _pick(F, (1024, 512, 256, 128))
    hb = grouped_mm(xs, [w1, w3], tile_e, n_used, tm=tm, tn=tn1, tk=D, epi="swiglu", out_dtype=BF16, name="moe_up")
    tn2 = _pick(D, (512, 256, 128))
    ys = grouped_mm(hb, [w2], tile_e, n_used, tm=tm, tn=tn2, tk=F, epi="none", out_dtype=F32, name="moe_down")
    pos2 = pos.reshape(rows, TOP_K)
    return moe_combine(ys, pos2[:, 0], pos2[:, 1], route, x, mod4, geo, gate_idx, rows, tg)


def _pack_w_in(w):
    D = w.shape[0]
    cols, at = [], 0
    for piece, off, width in sorted(_Z.values(), key=lambda v: v[1]):
        assert off == at
        lo, hi = _REF_OFF[piece], _REF_OFF[piece + 1]
        cols.append(w[:, lo:hi].astype(BF16))
        if width > hi - lo:
            cols.append(jnp.zeros((D, width - (hi - lo)), BF16))
        at += width
    assert at == Z_W
    return jnp.concatenate(cols, axis=1)


def kernel(x, c, ctx, c_ctx, ada_w, ada_b, norm1_g, norm2_g, w_in, mlstm_conv_w, mlstm_conv_b, mlstm_gate_b, mlstm_norm_g, conv_dw_w, conv_dw_b, conv_ln_g, conv_ln_b, gqa_q_norm_g, gqa_k_norm_g, diff_q_norm_g, diff_k_norm_g, diff_lambda, diff_subln_g, merge_gate_w, merge_gate_b, branch_w, out_w, ffn_w1, ffn_w3, ffn_w2, moe_router, moe_w1, moe_w3, moe_w2):
    B, S, D = x.shape
    L = ctx.shape[1]
    depth = ada_w.shape[0]
    geo = Geo(B, S, L, D)
    N, BS = geo.N, geo.BS
    assert S % GRID_W == 0 and S % M_CHUNK == 0 and L % M_CHUNK == 0 and BS % L == 0
    tb = _pick(math.gcd(S, L), (256, 128, 64))
    tm = _pick(math.gcd(S, B * L), (1024, 512, 256, 128))

    xt = jnp.concatenate([x.reshape(BS, D), ctx.reshape(B * L, D)], axis=0)
    cc = jnp.zeros((8, D), F32).at[:B].set(c).at[B].set(c_ctx)

    for l in range(depth):
        need_ctx = l < depth - 1
        rows = N if need_ctx else BS
        lam_init = 0.8 - 0.6 * math.exp(-0.3 * l)
        mod = adaln(cc, ada_w[l], ada_b[l])
        mod4 = mod[:B + 1].reshape(B + 1, 6, 1, D)

        h = norm_mod(xt, norm1_g[l], mod4, geo, 0, 1, N, BF16)
        z = dense_mm(h, [_pack_w_in(w_in[l])], rows=N, tm=tm, tn=_pick(Z_W, (1920, 640, 128)), tk=D, order="mn",
                     epi="none", out_dtype=F32, name="w_in")

        mq, mk = mlstm_qk(z, mlstm_conv_w[l], mlstm_conv_b[l], geo, tb)
        bc, cg = mlstm_gates(z, mlstm_gate_b[l], geo, tb)
        lanes = np.array([_gate_lane(d, hh) for d in range(2) for hh in range(M_HEADS)])
        cr = cg[:, lanes].reshape(N // M_CHUNK, M_CHUNK, 2 * M_HEADS).transpose(0, 2, 1)
        hf, hbw = mlstm_scan(mq, mk, z, bc, cg, cr, geo, tb)
        a_br = mlstm_out(hf, hbw, z, mlstm_norm_g[l], rows, tb)

        b_br = conformer(z, conv_dw_w[l], conv_dw_b[l], conv_ln_g[l], conv_ln_b[l], geo, tb)

        gq, gk, gv, xq, xk, xv = attn_prep(z, gqa_q_norm_g[l], gqa_k_norm_g[l], diff_q_norm_g[l], diff_k_norm_g[l],
                                           geo, tb)
        dkw = dict(lam=diff_lambda[l], sub_g=diff_subln_g[l], lam_init=lam_init)
        c_br = attention(gq, gk, gv, geo, n_kv=G_KV, d=G_HD, mode="gqa", ctx_queries=False)
        d_br = attention(xq, xk, xv, geo, n_kv=X_HEADS, d=X_HD, mode="diff", ctx_queries=False, **dkw)
        if need_ctx:
            c_ctx_o = attention(gq, gk, gv, geo, n_kv=G_KV, d=G_HD, mode="gqa", ctx_queries=True)
            d_ctx_o = attention(xq, xk, xv, geo, n_kv=X_HEADS, d=X_HD, mode="diff", ctx_queries=True, **dkw)
            c_br = jnp.concatenate([c_br, c_ctx_o], axis=0)
            d_br = jnp.concatenate([d_br, d_ctx_o], axis=0)

        tn = _pick(D, (512, 256, 128))
        y = merge(h, [a_br, b_br, c_br, d_br], merge_gate_w[l].astype(BF16), merge_gate_b[l].reshape(4, 1, D),
                  branch_w[l].astype(BF16), rows=rows, tm=tm, tn=tn)
        xt = dense_mm(y, [out_w[l].astype(BF16)], rows=rows, tm=tm, tn=tn, tk=D, order="mn", epi="resid",
                      out_dtype=F32, geo=geo, res=xt, mod4=mod4, gate_idx=2, name="out_proj")

        i = l // 2
        if l % 2 == 0:
            h2 = norm_mod(xt, norm2_g[l], mod4, geo, 3, 4, rows, BF16)
            F = ffn_w1.shape[2]
            tnf = _pick(F, (512, 256, 128))
            hb = dense_mm(h2, [ffn_w1[i].astype(BF16), ffn_w3[i].astype(BF16)], rows=rows, tm=tm, tn=tnf, tk=D,
                          order="mn", epi="swiglu", out_dtype=BF16, name="ffn_up")
            xt = dense_mm(hb, [ffn_w2[i].astype(BF16)], rows=rows, tm=tm, tn=tn, tk=F,
                          order="mn", epi="resid", out_dtype=F32, geo=geo, res=xt, mod4=mod4, gate_idx=5,
                          name="ffn_down")
        else:
            h2 = norm_mod(xt, norm2_g[l], mod4, geo, 3, 4, rows, F32)
            xt = moe_ffn(h2, xt, mod4, geo, 5, moe_router[i], moe_w1[i], moe_w3[i], moe_w2[i].astype(BF16), rows)
    return xt[:BS].reshape(B, S, D)
```
